```python
import jax, jax.numpy as jnp
from jax import lax
import numpy as np

D_MODEL = 1024
BATCH = 16
SEQ = 2048
DEPTH = 2

MEM_LEN = 256
N_MIXERS = 4
GROUP_WIDTH = D_MODEL // N_MIXERS
N_HEADS = 4
HEAD_DIM = GROUP_WIDTH // N_HEADS
GLA_KEY_DIM = HEAD_DIM // 2
GLA_LOWRANK = 16
GLA_GATE_NORM = 16.0
GDN_CONV = 4
CHUNK = 64
ROPE_BASE = 10000.0
XATTN_HEADS = 4
XATTN_HEAD_DIM = D_MODEL // XATTN_HEADS
D_FF = 2816
FFN_CONV = 3
EPS = 1e-6

IN_SPLITS = (
    GROUP_WIDTH, GROUP_WIDTH, GROUP_WIDTH, GROUP_WIDTH,
    GROUP_WIDTH, GROUP_WIDTH, GROUP_WIDTH, N_HEADS, N_HEADS, GROUP_WIDTH,
    N_HEADS * GLA_KEY_DIM, N_HEADS * GLA_KEY_DIM, GROUP_WIDTH, GLA_LOWRANK, GROUP_WIDTH,
    GROUP_WIDTH, GROUP_WIDTH, GROUP_WIDTH, GROUP_WIDTH,
)
IN_WIDTH = sum(IN_SPLITS)

kernel_name = 'hybrid_parallel_heads_decoder'


def _rmsnorm(x, w):
    xf = x.astype(jnp.float32)
    y = xf * lax.rsqrt(jnp.mean(xf * xf, axis=-1, keepdims=True) + EPS)
    return (y * w.astype(jnp.float32)).astype(x.dtype)


def _norm_f32(x):
    return x * lax.rsqrt(jnp.mean(x * x, axis=-1, keepdims=True) + EPS)


def _l2norm(x):
    return x * lax.rsqrt(jnp.sum(x * x, axis=-1, keepdims=True) + EPS)


def _rope(x, positions):
    d = x.shape[-1]
    freqs = ROPE_BASE ** (-jnp.arange(0, d, 2, dtype=jnp.float32) / d)
    ang = positions.astype(jnp.float32)[..., None] * freqs
    cos, sin = jnp.cos(ang)[:, :, None, :], jnp.sin(ang)[:, :, None, :]
    x1, x2 = jnp.split(x, 2, axis=-1)
    return jnp.concatenate([x1 * cos - x2 * sin, x1 * sin + x2 * cos], axis=-1)


def _causal_dwconv(x, w):
    k, c = w.shape
    return lax.conv_general_dilated(
        x, w[:, None, :].astype(x.dtype), window_strides=(1,), padding=[(k - 1, 0)],
        dimension_numbers=('NWC', 'WIO', 'NWC'), feature_group_count=c)


def _to_chunks(t):
    b, tt, h = t.shape[:3]
    t = t.reshape((b, tt // CHUNK, CHUNK, h) + t.shape[3:])
    return jnp.moveaxis(t, 3, 1)


def _from_chunks(t):
    b, h, n, c, d = t.shape
    return jnp.moveaxis(t, 1, 3).reshape(b, n * c, h, d)


def _retention_chunked(q, k, v, log_gamma):
    c = q.shape[3]
    idx = jnp.arange(c, dtype=jnp.float32)
    lg = log_gamma[:, None]
    rel = idx[:, None] - idx[None, :]
    decay = jnp.where(rel >= 0, jnp.exp(lg[:, :, None] * jnp.maximum(rel, 0.0)), 0.0)
    scores = jnp.einsum('bhncd,bhnsd->bhncs', q, k) * decay[None, :, None]
    o_intra = jnp.einsum('bhncs,bhnsv->bhncv', scores, v)
    q_decay = jnp.exp(lg * (idx + 1.0))
    k_decay = jnp.exp(lg * (c - 1.0 - idx))
    chunk_decay = jnp.exp(log_gamma * c)
    kv = jnp.einsum('bhncd,bhncv->bhndv', k * k_decay[None, :, None, :, None], v)

    def step(state, inp):
        kv_n, q_n = inp
        o = jnp.einsum('bhcd,bhdv->bhcv', q_n, state) * q_decay[None, :, :, None]
        state = state * chunk_decay[None, :, None, None] + kv_n
        return state, o

    b, h, _, _, dk = q.shape
    s0 = jnp.zeros((b, h, dk, v.shape[-1]), jnp.float32)
    _, o_inter = lax.scan(step, s0, (jnp.moveaxis(kv, 2, 0), jnp.moveaxis(q, 2, 0)))
    return o_intra + jnp.moveaxis(o_inter, 0, 2)


def _gated_delta_chunked(q, k, v, beta, g):
    c = q.shape[3]
    gc = jnp.cumsum(g, axis=-1)
    causal = jnp.tril(jnp.ones((c, c), dtype=bool))
    strict = jnp.tril(jnp.ones((c, c), dtype=bool), k=-1)
    ratio = jnp.exp(jnp.where(causal, gc[..., :, None] - gc[..., None, :], -jnp.inf))
    kk = jnp.where(strict, jnp.einsum('bhncd,bhnsd->bhncs', k, k) * ratio, 0.0)
    a_mat = jnp.eye(c, dtype=jnp.float32) + beta[..., None] * kk
    gamma = jnp.exp(gc)
    u_v = lax.linalg.triangular_solve(a_mat, beta[..., None] * v, left_side=True, lower=True, unit_diagonal=True)
    w_k = lax.linalg.triangular_solve(a_mat, (beta * gamma)[..., None] * k, left_side=True, lower=True, unit_diagonal=True)
    qk = jnp.einsum('bhncd,bhnsd->bhncs', q, k) * ratio
    q_g = q * gamma[..., None]
    k_tail = k * jnp.exp(gc[..., -1:] - gc)[..., None]
    chunk_decay = jnp.exp(gc[..., -1])

    def step(state, inp):
        uv_n, w_n, qk_n, qg_n, kt_n, cd_n = inp
        u = uv_n - jnp.einsum('bhcd,bhdv->bhcv', w_n, state)
        o = jnp.einsum('bhcd,bhdv->bhcv', qg_n, state) + jnp.einsum('bhcs,bhsv->bhcv', qk_n, u)
        state = cd_n[..., None, None] * state + jnp.einsum('bhcd,bhcv->bhdv', kt_n, u)
        return state, o

    b, h, _, _, dk = q.shape
    s0 = jnp.zeros((b, h, dk, v.shape[-1]), jnp.float32)
    xs = tuple(jnp.moveaxis(t, 2, 0) for t in (u_v, w_k, qk, q_g, k_tail, chunk_decay))
    _, o = lax.scan(step, s0, xs)
    return jnp.moveaxis(o, 0, 2)


def _gated_linear_chunked(q, k, v, log_f):
    c = q.shape[3]
    gc = jnp.cumsum(log_f, axis=3)
    causal = jnp.tril(jnp.ones((c, c), dtype=bool))[:, :, None]

    def step(state, inp):
        q_n, k_n, v_n, g_n = inp
        diff = g_n[:, :, :, None, :] - g_n[:, :, None, :, :]
        decay = jnp.exp(jnp.where(causal, diff, -jnp.inf))
        scores = jnp.einsum('bhcd,bhsd,bhcsd->bhcs', q_n, k_n, decay)
        o = jnp.einsum('bhcs,bhsv->bhcv', scores, v_n) + jnp.einsum('bhcd,bhdv->bhcv', q_n * jnp.exp(g_n), state)
        k_tail = k_n * jnp.exp(g_n[:, :, -1:] - g_n)
        state = jnp.exp(g_n[:, :, -1])[..., None] * state + jnp.einsum('bhcd,bhcv->bhdv', k_tail, v_n)
        return state, o

    b, h, _, _, dk = q.shape
    s0 = jnp.zeros((b, h, dk, v.shape[-1]), jnp.float32)
    xs = tuple(jnp.moveaxis(t, 2, 0) for t in (q, k, v, gc))
    _, o = lax.scan(step, s0, xs)
    return jnp.moveaxis(o, 0, 2)


def _hybrid_mixer(h, positions, w_in, gdn_conv_w, gdn_a_log, gdn_dt_bias, gdn_norm_w,
                  gla_gk_up, gla_gk_bias, gla_norm_w, hgrn_lb, hgrn_norm_w, w_out):
    b, t, _ = h.shape
    f32 = jnp.float32
    z = jnp.matmul(h, w_in).astype(f32)
    split_points = [int(p) for p in np.cumsum(IN_SPLITS)[:-1]]
    (rq, rk, rv, rg, bq, bk, bv, ba, bb, bg,
     cq, ck, cv, c_lr, cg, dq, df, di, dg) = jnp.split(z, split_points, axis=-1)

    def heads(a):
        return a.reshape(b, t, N_HEADS, -1)

    q_a = _rope(heads(rq), positions)
    k_a = _rope(heads(rk), positions) * HEAD_DIM ** -0.5
    log_gamma = jnp.log(1.0 - jnp.exp2(-5.0 - jnp.arange(N_HEADS, dtype=f32)))
    o_a = _from_chunks(_retention_chunked(_to_chunks(q_a), _to_chunks(k_a), _to_chunks(heads(rv)), log_gamma))
    o_a = _norm_f32(o_a) * jax.nn.silu(heads(rg))

    qkv = jax.nn.silu(_causal_dwconv(jnp.concatenate([bq, bk, bv], axis=-1), gdn_conv_w.astype(f32)))
    q_b, k_b, v_b = jnp.split(qkv, 3, axis=-1)
    q_b = _l2norm(heads(q_b)) * HEAD_DIM ** -0.5
    k_b = _l2norm(heads(k_b))
    beta = jax.nn.sigmoid(bb)
    g_b = -jnp.exp(gdn_a_log.astype(f32)) * jax.nn.softplus(ba + gdn_dt_bias.astype(f32))
    o_b = _from_chunks(_gated_delta_chunked(_to_chunks(q_b), _to_chunks(k_b), _to_chunks(heads(v_b)),
                                            _to_chunks(beta), _to_chunks(g_b)))
    o_b = _norm_f32(o_b) * gdn_norm_w.astype(f32) * jax.nn.silu(heads(bg))

    q_c = heads(cq) * GLA_KEY_DIM ** -0.5
    log_f_c = jax.nn.log_sigmoid(jnp.matmul(c_lr, gla_gk_up.astype(f32)) + gla_gk_bias.astype(f32)) / GLA_GATE_NORM
    o_c = _from_chunks(_gated_linear_chunked(_to_chunks(q_c), _to_chunks(heads(ck)), _to_chunks(heads(cv)),
                                             _to_chunks(heads(log_f_c))))
    o_c = _norm_f32(o_c) * gla_norm_w.astype(f32) * jax.nn.silu(heads(cg))

    lb = hgrn_lb.reshape(N_HEADS, HEAD_DIM)
    f_pre = heads(df)
    log_f_d = jnp.logaddexp(jnp.log(lb), jnp.log1p(-lb) + jax.nn.log_sigmoid(f_pre))
    k_d = (1.0 - lb) * jax.nn.sigmoid(-f_pre)
    o_d = _from_chunks(_gated_linear_chunked(_to_chunks(heads(dq)), _to_chunks(k_d), _to_chunks(heads(di)),
                                             _to_chunks(log_f_d)))
    o_d = _norm_f32(o_d) * hgrn_norm_w.astype(f32) * jax.nn.silu(heads(dg))

    o = jnp.concatenate([o_a.reshape(b, t, GROUP_WIDTH), o_b.reshape(b, t, GROUP_WIDTH),
                         o_c.reshape(b, t, GROUP_WIDTH), o_d.reshape(b, t, GROUP_WIDTH)], axis=-1)
    return jnp.matmul(o.astype(h.dtype), w_out)


def _memory_cross_attn(h, mem_n, wq, wk, wv, wo):
    b, t, _ = h.shape
    m = mem_n.shape[1]
    q = jnp.matmul(h, wq).reshape(b, t, XATTN_HEADS, XATTN_HEAD_DIM)
    k = jnp.matmul(mem_n, wk).reshape(b, m, XATTN_HEADS, XATTN_HEAD_DIM)
    v = jnp.matmul(mem_n, wv).reshape(b, m, XATTN_HEADS, XATTN_HEAD_DIM)
    s = jnp.einsum('bthd,bmhd->bhtm', q, k).astype(jnp.float32) * XATTN_HEAD_DIM ** -0.5
    p = jax.nn.softmax(s, axis=-1).astype(v.dtype)
    o = jnp.einsum('bhtm,bmhd->bthd', p, v).reshape(b, t, D_MODEL)
    return jnp.matmul(o, wo)


def _conv_ffn(h, w_up, conv_w, w_down):
    u = _causal_dwconv(jnp.matmul(h, w_up), conv_w)
    gate, val = jnp.split(u, 2, axis=-1)
    return jnp.matmul(jax.nn.silu(gate) * val, w_down)


def setup_inputs(seed: int = 0) -> dict:
    key = jax.random.key(seed)
    ks = jax.random.split(key, 26)
    nrm = jax.random.normal

    def gain(k, shape):
        return 1.0 + 0.02 * nrm(k, shape, jnp.float32)

    dt = jnp.exp(jax.random.uniform(ks[6], (DEPTH, N_HEADS), jnp.float32, np.log(1e-3), np.log(1e-1)))
    pos_offset = jax.random.randint(ks[2], (BATCH, 1), 0, 4096, dtype=jnp.int32)
    return {
        'x': nrm(ks[0], (BATCH, SEQ, D_MODEL), jnp.float32),
        'mem': nrm(ks[1], (BATCH, MEM_LEN, D_MODEL), jnp.float32),
        'positions': pos_offset + jnp.arange(SEQ, dtype=jnp.int32)[None, :],
        'mix_norm_w': gain(ks[3], (DEPTH, D_MODEL)),
        'w_in': nrm(ks[4], (DEPTH, D_MODEL, IN_WIDTH), jnp.float32) * D_MODEL ** -0.5,
        'gdn_conv_w': nrm(ks[5], (DEPTH, GDN_CONV, 3 * GROUP_WIDTH), jnp.float32) * GDN_CONV ** -0.5,
        'gdn_a_log': jnp.log(jax.random.uniform(ks[7], (DEPTH, N_HEADS), jnp.float32, 1.0, 16.0)),
        'gdn_dt_bias': dt + jnp.log(-jnp.expm1(-dt)),
        'gdn_norm_w': gain(ks[8], (DEPTH, HEAD_DIM)),
        'gla_gk_up': nrm(ks[9], (DEPTH, GLA_LOWRANK, N_HEADS * GLA_KEY_DIM), jnp.float32) * GLA_LOWRANK ** -0.5,
        'gla_gk_bias': 0.02 * nrm(ks[10], (DEPTH, N_HEADS * GLA_KEY_DIM), jnp.float32),
        'gla_norm_w': gain(ks[11], (DEPTH, HEAD_DIM)),
        'hgrn_lb_logits': 0.1 * nrm(ks[12], (DEPTH, GROUP_WIDTH), jnp.float32),
        'hgrn_norm_w': gain(ks[13], (DEPTH, HEAD_DIM)),
        'w_out': nrm(ks[14], (DEPTH, D_MODEL, D_MODEL), jnp.float32) * D_MODEL ** -0.5,
        'xattn_norm_w': gain(ks[15], (DEPTH, D_MODEL)),
        'mem_norm_w': gain(ks[16], (D_MODEL,)),
        'xattn_wq': nrm(ks[17], (DEPTH, D_MODEL, D_MODEL), jnp.float32) * D_MODEL ** -0.5,
        'xattn_wk': nrm(ks[18], (DEPTH, D_MODEL, D_MODEL), jnp.float32) * D_MODEL ** -0.5,
        'xattn_wv': nrm(ks[19], (DEPTH, D_MODEL, D_MODEL), jnp.float32) * D_MODEL ** -0.5,
        'xattn_wo': nrm(ks[20], (DEPTH, D_MODEL, D_MODEL), jnp.float32) * D_MODEL ** -0.5,
        'ffn_norm_w': gain(ks[21], (DEPTH, D_MODEL)),
        'ffn_up': nrm(ks[22], (DEPTH, D_MODEL, 2 * D_FF), jnp.float32) * D_MODEL ** -0.5,
        'ffn_conv_w': nrm(ks[23], (DEPTH, FFN_CONV, 2 * D_FF), jnp.float32) * FFN_CONV ** -0.5,
        'ffn_down': nrm(ks[24], (DEPTH, D_FF, D_MODEL), jnp.float32) * D_FF ** -0.5,
        'final_norm_w': gain(ks[25], (D_MODEL,)),
    }


def reference(x, mem, positions, mix_norm_w, w_in, gdn_conv_w, gdn_a_log, gdn_dt_bias, gdn_norm_w,
              gla_gk_up, gla_gk_bias, gla_norm_w, hgrn_lb_logits, hgrn_norm_w, w_out,
              xattn_norm_w, mem_norm_w, xattn_wq, xattn_wk, xattn_wv, xattn_wo,
              ffn_norm_w, ffn_up, ffn_conv_w, ffn_down, final_norm_w):
    lb_all = jnp.cumsum(jax.nn.softmax(hgrn_lb_logits.astype(jnp.float32), axis=0), axis=0)
    lb_all = lb_all - lb_all[0]
    mem_n = _rmsnorm(mem, mem_norm_w)
    h = x
    for l in range(DEPTH):
        h = h + _hybrid_mixer(_rmsnorm(h, mix_norm_w[l]), positions, w_in[l], gdn_conv_w[l], gdn_a_log[l],
                              gdn_dt_bias[l], gdn_norm_w[l], gla_gk_up[l], gla_gk_bias[l], gla_norm_w[l],
                              lb_all[l], hgrn_norm_w[l], w_out[l])
        h = h + _memory_cross_attn(_rmsnorm(h, xattn_norm_w[l]), mem_n, xattn_wq[l], xattn_wk[l],
                                   xattn_wv[l], xattn_wo[l])
        h = h + _conv_ffn(_rmsnorm(h, ffn_norm_w[l]), ffn_up[l], ffn_conv_w[l], ffn_down[l])
    return _rmsnorm(h, final_norm_w)
```

```python
import functools

import numpy as np
import jax
import jax.numpy as jnp
from jax import lax
from jax.experimental import pallas as pl
from jax.experimental.pallas import tpu as pltpu

F32 = jnp.float32
BF16 = jnp.bfloat16

N_HEADS = 4
HEAD_DIM = 64
GROUP = N_HEADS * HEAD_DIM
GLA_KEY_DIM = 32
GLA_KEYS = N_HEADS * GLA_KEY_DIM
GLA_LOWRANK = 16
GLA_GATE_NORM = 16.0
GDN_CONV = 4
CHUNK = 64
ROPE_BASE = 10000.0
ROPE_HALF = HEAD_DIM // 2
XATTN_HEADS = 4
FFN_CONV = 3
EPS = 1e-6

LANES = 128
VMEM_LIMIT = 52 * 1024 * 1024

Z_RQ, Z_RK, Z_RV, Z_RG = 0, 256, 512, 768
Z_BQKV, Z_BG = 1024, 1792
Z_CQ, Z_CK, Z_CV, Z_CG = 2048, 2176, 2304, 2560
Z_DQ, Z_DF, Z_DI, Z_DG = 2816, 3072, 3328, 3584
Z_SMALL = 3840
Z_WIDTH = 4096
SM_BA, SM_BB, SM_LR = 0, 4, 8

LEVELS = (1, 2, 4, 8, 16, 32)


def _bf(x):
    return x.astype(BF16)


def _dot(a, b):
    return jnp.dot(a, b, preferred_element_type=F32)


def _dot_nt(a, b):
    return lax.dot_general(a, b, (((1,), (1,)), ((), ())), preferred_element_type=F32)


def _dot_tn(a, b):
    return lax.dot_general(a, b, (((0,), (0,)), ((), ())), preferred_element_type=F32)


def _split(x, n):
    parts = []
    r = x
    for i in range(n):
        p = r.astype(BF16)
        parts.append(p)
        if i < n - 1:
            r = r - p.astype(F32)
    return parts


def _sel_dot(c, x, n=3):
    acc = None
    for p in _split(x, n):
        t = _dot(c, p)
        acc = t if acc is None else acc + t
    return acc


def _dot_sel(x, c, n=3):
    acc = None
    for p in _split(x, n):
        t = _dot(p, c)
        acc = t if acc is None else acc + t
    return acc


def _dot_hp(a, b, dotfn=_dot):
    a_hi, a_lo = _split(a, 2)
    b_hi, b_lo = _split(b, 2)
    return dotfn(a_hi, b_hi) + dotfn(a_lo, b_hi) + dotfn(a_hi, b_lo)


def _silu(x):
    return x * jax.nn.sigmoid(x)


def _log_sigmoid(x):
    return jnp.minimum(x, 0.0) - jnp.log1p(jnp.exp(-jnp.abs(x)))


def _softplus(x):
    return jnp.maximum(x, 0.0) + jnp.log1p(jnp.exp(-jnp.abs(x)))


def _rms(x, w):
    ms = jnp.mean(x * x, axis=-1, keepdims=True)
    return x * lax.rsqrt(ms + EPS) * w


def _blockdiag(x, mask):
    return jnp.concatenate([x, x, x, x], axis=0) * mask


def _norm_matmul_body(x_ref, nw_ref, w_ref, o_ref, xn_ref):
    @pl.when(pl.program_id(1) == 0)
    def _():
        xn_ref[...] = _bf(_rms(x_ref[...], nw_ref[...]))

    o_ref[...] = _dot(xn_ref[...], w_ref[...]).astype(o_ref.dtype)


def _norm_matmul(x, nw, w, out_dtype, tm, tn):
    m, d = x.shape
    n = w.shape[1]
    return pl.pallas_call(
        _norm_matmul_body,
        out_shape=jax.ShapeDtypeStruct((m, n), out_dtype),
        grid=(m // tm, n // tn),
        in_specs=[
            pl.BlockSpec((tm, d), lambda i, j: (i, 0)),
            pl.BlockSpec((1, d), lambda i, j: (0, 0)),
            pl.BlockSpec((d, tn), lambda i, j: (0, j)),
        ],
        out_specs=pl.BlockSpec((tm, tn), lambda i, j: (i, j)),
        scratch_shapes=[pltpu.VMEM((tm, d), BF16)],
        compiler_params=pltpu.CompilerParams(
            dimension_semantics=("parallel", "arbitrary"),
            vmem_limit_bytes=VMEM_LIMIT),
        name="norm_matmul",
    )(x, nw.reshape(1, d), w)


def _matmul_residual_body(a_ref, w_ref, r_ref, o_ref):
    o_ref[...] = r_ref[...] + _dot(a_ref[...], w_ref[...])


def _matmul_residual(a, w, res, tm):
    m, k = a.shape
    n = w.shape[1]
    return pl.pallas_call(
        _matmul_residual_body,
        out_shape=jax.ShapeDtypeStruct((m, n), F32),
        grid=(m // tm,),
        in_specs=[
            pl.BlockSpec((tm, k), lambda i: (i, 0)),
            pl.BlockSpec((k, n), lambda i: (0, 0)),
            pl.BlockSpec((tm, n), lambda i: (i, 0)),
        ],
        out_specs=pl.BlockSpec((tm, n), lambda i: (i, 0)),
        compiler_params=pltpu.CompilerParams(
            dimension_semantics=("parallel",),
            vmem_limit_bytes=VMEM_LIMIT),
        name="matmul_residual",
    )(a, w, res)


def _rope_table_body(pos_ref, freq_ref, cos_ref, sin_ref):
    ang = pos_ref[...].astype(F32) * freq_ref[...]
    cos_ref[...] = jnp.cos(ang)
    sin_ref[...] = jnp.sin(ang)


def _rope_tables(pos, freq_row, tm):
    m = pos.shape[0]
    return pl.pallas_call(
        _rope_table_body,
        out_shape=(jax.ShapeDtypeStruct((m, LANES), F32),
                   jax.ShapeDtypeStruct((m, LANES), F32)),
        grid=(m // tm,),
        in_specs=[
            pl.BlockSpec((tm, 1), lambda i: (i, 0)),
            pl.BlockSpec((1, LANES), lambda i: (0, 0)),
        ],
        out_specs=(pl.BlockSpec((tm, LANES), lambda i: (i, 0)),
                   pl.BlockSpec((tm, LANES), lambda i: (i, 0))),
        compiler_params=pltpu.CompilerParams(dimension_semantics=("parallel",)),
        name="rope_tables",
    )(pos, freq_row)


def _xattn_body(h_ref, nw_ref, wq_ref, k_ref, v_ref, wo_ref, o_ref, att_ref):
    h = h_ref[...]
    d = h.shape[-1]
    hd = d // XATTN_HEADS
    hn = _bf(_rms(h, nw_ref[...]))
    q = _bf(_dot(hn, wq_ref[...]) * (hd ** -0.5))
    for i in range(XATTN_HEADS):
        cs = slice(i * hd, (i + 1) * hd)
        s = _dot_nt(q[:, cs], k_ref[0, :, cs])
        s = s - jnp.max(s, axis=-1, keepdims=True)
        p = jnp.exp(s)
        p = p / jnp.sum(p, axis=-1, keepdims=True)
        att_ref[:, cs] = _bf(_dot(_bf(p), v_ref[0, :, cs]))
    o_ref[...] = h + _dot(att_ref[...], wo_ref[...])


def _xattn(h, nw, wq, kv, wo, batch, tq):
    m, d = h.shape
    t = m // batch
    mem_len = kv.shape[1]
    nt = t // tq
    return pl.pallas_call(
        _xattn_body,
        out_shape=jax.ShapeDtypeStruct((m, d), F32),
        grid=(batch, nt),
        in_specs=[
            pl.BlockSpec((tq, d), lambda b, i: (b * nt + i, 0)),
            pl.BlockSpec((1, d), lambda b, i: (0, 0)),
            pl.BlockSpec((d, d), lambda b, i: (0, 0)),
            pl.BlockSpec((1, mem_len, d), lambda b, i: (b, 0, 0)),
            pl.BlockSpec((1, mem_len, d), lambda b, i: (b, 0, 1)),
            pl.BlockSpec((d, d), lambda b, i: (0, 0)),
        ],
        out_specs=pl.BlockSpec((tq, d), lambda b, i: (b * nt + i, 0)),
        scratch_shapes=[pltpu.VMEM((tq, d), BF16)],
        compiler_params=pltpu.CompilerParams(
            dimension_semantics=("parallel", "parallel"),
            vmem_limit_bytes=VMEM_LIMIT),
        name="xattn",
    )(h, nw.reshape(1, d), wq, kv, kv, wo)


def _shift_rows(x, k, tail, row):
    y = pltpu.roll(x, k, 0)
    for j in range(k):
        src = tail.shape[0] - k + j
        y = jnp.where(row == j, tail[src:src + 1, :], y)
    return y


def _conv_ffn_down_body(u_ref, halo_ref, cw_ref, wd_ref, r_ref, fw_ref, o_ref, act_ref,
                        *, tiles_per_seq, col_chunk, final_norm):
    tm = u_ref.shape[0]
    d_ff = act_ref.shape[1]
    first = (pl.program_id(0) % tiles_per_seq) == 0
    keep = jnp.where(first, 0.0, 1.0)
    row = lax.broadcasted_iota(jnp.int32, (tm, 1), 0)
    hrows = halo_ref.shape[0]

    def conv(c0):
        cs = slice(c0, c0 + col_chunk)
        x = u_ref[:, cs].astype(F32)
        tail = halo_ref[hrows - 8:hrows, cs].astype(F32) * keep
        w = cw_ref[:, cs]
        y = x * w[2:3, :]
        y = y + _shift_rows(x, 1, tail, row) * w[1:2, :]
        y = y + _shift_rows(x, 2, tail, row) * w[0:1, :]
        return y

    for c0 in range(0, d_ff, col_chunk):
        gate = conv(c0)
        val = conv(d_ff + c0)
        act_ref[:, c0:c0 + col_chunk] = _bf(_silu(gate) * val)
    out = r_ref[...] + _dot(act_ref[...], wd_ref[...])
    if final_norm:
        out = _rms(out, fw_ref[...])
    o_ref[...] = out


def _conv_ffn_down(u, conv_w, w_down, res, final_w, seq_len, tm, final_norm):
    m, n2 = u.shape
    d_ff = n2 // 2
    d = w_down.shape[1]
    halo = 16
    col_chunk = 256
    body = functools.partial(_conv_ffn_down_body, tiles_per_seq=seq_len // tm,
                             col_chunk=col_chunk, final_norm=final_norm)
    return pl.pallas_call(
        body,
        out_shape=jax.ShapeDtypeStruct((m, d), F32),
        grid=(m // tm,),
        in_specs=[
            pl.BlockSpec((tm, n2), lambda i: (i, 0)),
            pl.BlockSpec((halo, n2), lambda i: (jnp.maximum(i * (tm // halo) - 1, 0), 0)),
            pl.BlockSpec((FFN_CONV, n2), lambda i: (0, 0)),
            pl.BlockSpec((d_ff, d), lambda i: (0, 0)),
            pl.BlockSpec((tm, d), lambda i: (i, 0)),
            pl.BlockSpec((1, d), lambda i: (0, 0)),
        ],
        out_specs=pl.BlockSpec((tm, d), lambda i: (i, 0)),
        scratch_shapes=[pltpu.VMEM((tm, d_ff), BF16)],
        compiler_params=pltpu.CompilerParams(
            dimension_semantics=("parallel",),
            vmem_limit_bytes=VMEM_LIMIT),
        name="conv_ffn_down",
    )(u, u, conv_w, w_down, res, final_w.reshape(1, d))


def _mixer_constants():
    c = CHUNK
    ar = np.arange
    wide_head = ar(GROUP) // c
    std_head = ar(GROUP) // HEAD_DIM
    rope_head = (ar(GROUP) % LANES) // ROPE_HALF
    gla_head = ar(GLA_KEYS) // GLA_KEY_DIM
    row_head = ar(GROUP) // c
    s_of = ar(GROUP) % c
    t = ar(c)

    consts = {}
    consts["bd_std"] = (row_head[:, None] == std_head[None, :])
    consts["bd_rope"] = (row_head[:, None] == rope_head[None, :])
    consts["bd_gla"] = (row_head[:, None] == gla_head[None, :])
    consts["s_rope"] = (rope_head[:, None] == std_head[None, :])
    consts["s_gla"] = (gla_head[:, None] == std_head[None, :])

    log_gamma = np.log(1.0 - np.exp2(-5.0 - ar(N_HEADS, dtype=np.float64)))
    scale = HEAD_DIM ** -0.5
    rel = t[:, None] - s_of[None, :]
    lg_w = log_gamma[wide_head][None, :]
    consts["ret_d"] = np.where(rel >= 0, np.exp(lg_w * np.maximum(rel, 0)), 0.0) * scale
    lg_r = log_gamma[rope_head][None, :]
    consts["ret_qdec"] = np.exp(lg_r * (t[:, None] + 1.0))
    consts["ret_kdec"] = np.exp(lg_r * (c - 1.0 - t[:, None])) * scale
    consts["ret_cdec"] = np.broadcast_to(np.exp(log_gamma[rope_head] * c)[:, None], (GROUP, GROUP))

    lvl = []
    sel_q, sel_k = [], []
    for b in LEVELS:
        blk = t // (2 * b)
        second = (t % (2 * b)) >= b
        bound = blk * 2 * b + b - 1
        m = (blk[:, None] == blk[None, :]) & second[:, None] & (~second)[None, :]
        lvl.append(m[:, s_of])
        sel_q.append(second[:, None] & (t[None, :] > bound[:, None]) & (t[None, :] <= t[:, None]))
        sel_k.append((~second)[:, None] & (t[None, :] > t[:, None]) & (t[None, :] <= bound[:, None]))
    lvl.append((t[:, None] == s_of[None, :]))
    consts["lvl"] = np.stack(lvl)
    tri = t[None, :] <= t[:, None]
    tail = t[None, :] > t[:, None]
    consts["gl_sel"] = np.concatenate(sel_q + sel_k + [tri, tail], axis=0)
    consts["tri"] = tri
    consts["causal_w"] = (t[:, None] >= s_of[None, :])
    consts["strict_w"] = (t[:, None] > s_of[None, :])
    consts["eye_w"] = (t[:, None] == s_of[None, :])
    consts["ones_cw"] = np.ones((c, GROUP))
    consts["ones_cc"] = np.ones((c, c))
    ea = np.zeros((LANES, GROUP))
    eb = np.zeros((LANES, GROUP))
    ea[SM_BA + std_head, ar(GROUP)] = 1.0
    eb[SM_BB + std_head, ar(GROUP)] = 1.0
    consts["exp_a"] = ea
    consts["exp_b"] = eb
    return consts


_F32_CONSTS = ("ret_d", "ret_qdec", "ret_kdec", "ret_cdec", "lvl", "causal_w", "strict_w",
               "eye_w", "s_rope", "s_gla")
_CONST_ORDER = ("bd_std", "bd_rope", "bd_gla", "s_rope", "s_gla", "ret_d", "ret_qdec",
                "ret_kdec", "ret_cdec", "lvl", "gl_sel", "tri", "causal_w", "strict_w",
                "eye_w", "ones_cw", "ones_cc", "exp_a", "exp_b")
_PARAM_ORDER = ("gdn_conv", "gdn_alog", "gdn_dt", "gdn_nw", "gla_up", "gla_bias", "gla_nw",
                "hg_loglb", "hg_log1mlb", "hg_1mlb", "hg_nw")


def _mixer_body(*refs, n_chunks):
    z_ref, cos_ref, sin_ref = refs[:3]
    nc, npar = len(_CONST_ORDER), len(_PARAM_ORDER)
    cst = dict(zip(_CONST_ORDER, refs[3:3 + nc]))
    par = dict(zip(_PARAM_ORDER, refs[3 + nc:3 + nc + npar]))
    o_ref = refs[3 + nc + npar]
    sa_ref, sb_ref, sc_ref, sd_ref, tail_ref, qkv_ref = refs[4 + nc + npar:]
    tb = z_ref.shape[0]

    @pl.when(pl.program_id(1) == 0)
    def _():
        sa_ref[...] = jnp.zeros_like(sa_ref)
        sb_ref[...] = jnp.zeros_like(sb_ref)
        sc_ref[...] = jnp.zeros_like(sc_ref)
        sd_ref[...] = jnp.zeros_like(sd_ref)
        tail_ref[...] = jnp.zeros_like(tail_ref)

    row = lax.broadcasted_iota(jnp.int32, (tb, 1), 0)
    x = z_ref[:, Z_BQKV:Z_BQKV + 3 * GROUP]
    tail = tail_ref[...]
    cw = par["gdn_conv"][...]
    y = x * cw[GDN_CONV - 1:GDN_CONV, :]
    for k in range(1, GDN_CONV):
        y = y + _shift_rows(x, k, tail, row) * cw[GDN_CONV - 1 - k:GDN_CONV - k, :]
    qkv_ref[...] = _silu(y)
    tail_ref[...] = x[tb - 8:tb, :]

    bd_std = cst["bd_std"]
    inv_hd = 1.0 / HEAD_DIM

    def seg_sum(x):
        return _dot_sel(x, bd_std[...], 2)

    def head_norm(x):
        return x * lax.rsqrt(seg_sum(x * x) * inv_hd + EPS)

    def wide_matmul(a, b, passes):
        if passes == 1:
            return _dot(_bf(a), _blockdiag(_bf(b), bd_std[...]))
        a_hi, a_lo = _split(a, 2)
        b_hi, b_lo = _split(b, 2)
        bh = _blockdiag(b_hi, bd_std[...])
        bl = _blockdiag(b_lo, bd_std[...])
        return _dot(a_hi, bh) + _dot(a_lo, bh) + _dot(a_hi, bl)

    def gated_linear(q, k, v, logf, s_ref, bd_k, s_mask):
        parts = _split(logf, 3)
        sel = cst["gl_sel"][...]
        e = _dot(sel, parts[0]) + _dot(sel, parts[1]) + _dot(sel, parts[2])
        ex = jnp.exp(e)
        nl = len(LEVELS)
        p = None
        for l in range(nl + 1):
            if l < nl:
                ql = q * ex[l * CHUNK:(l + 1) * CHUNK]
                kl = k * ex[(nl + l) * CHUNK:(nl + l + 1) * CHUNK]
            else:
                ql, kl = q, k
            pl_ = _dot_nt(_bf(ql), _blockdiag(_bf(kl), bd_k[...])) * cst["lvl"][l]
            p = pl_ if p is None else p + pl_
        e_g = ex[2 * nl * CHUNK:(2 * nl + 1) * CHUNK]
        e_t = ex[(2 * nl + 1) * CHUNK:(2 * nl + 2) * CHUNK]
        vb = _bf(v)
        s = s_ref[...]
        o = _dot(_bf(p), _blockdiag(vb, bd_std[...])) + _dot(_bf(q * e_g), _bf(s))
        ones = cst["ones_cw"][...]
        cd = jnp.exp(_dot_tn(parts[0], ones) + _dot_tn(parts[1], ones) + _dot_tn(parts[2], ones))
        s_ref[...] = s * cd + s_mask[...] * _dot_tn(_bf(k * e_t), vb)
        return o

    def chunk(n, carry):
        r0 = pl.multiple_of(n * CHUNK, CHUNK)
        rs = pl.ds(r0, CHUNK)

        cos = cos_ref[rs, :]
        sin = sin_ref[rs, :]

        def rope(col):
            x1 = z_ref[rs, col:col + LANES]
            x2 = z_ref[rs, col + LANES:col + 2 * LANES]
            return jnp.concatenate([x1 * cos - x2 * sin, x1 * sin + x2 * cos], axis=1)

        qr = rope(Z_RQ)
        kr = rope(Z_RK)
        vb = _bf(z_ref[rs, Z_RV:Z_RV + GROUP])
        p = _dot_nt(_bf(qr), _blockdiag(_bf(kr), cst["bd_rope"][...])) * cst["ret_d"][...]
        s = sa_ref[...]
        o = _dot(_bf(p), _blockdiag(vb, bd_std[...]))
        o = o + _dot(_bf(qr * cst["ret_qdec"][...]), _bf(s))
        sa_ref[...] = (s * cst["ret_cdec"][...]
                       + cst["s_rope"][...] * _dot_tn(_bf(kr * cst["ret_kdec"][...]), vb))
        o = head_norm(o) * _silu(z_ref[rs, Z_RG:Z_RG + GROUP])
        o_ref[rs, 0:GROUP] = _bf(o)

        sm = z_ref[rs, Z_SMALL:Z_SMALL + LANES]
        sm_parts = _split(sm, 3)
        ea = cst["exp_a"][...]
        eb = cst["exp_b"][...]
        ba = _dot(sm_parts[0], ea) + _dot(sm_parts[1], ea) + _dot(sm_parts[2], ea)
        bb = _dot(sm_parts[0], eb) + _dot(sm_parts[1], eb) + _dot(sm_parts[2], eb)
        beta = jax.nn.sigmoid(bb)
        g = -jnp.exp(par["gdn_alog"][...]) * _softplus(ba + par["gdn_dt"][...])
        qb = qkv_ref[rs, 0:GROUP]
        kb = qkv_ref[rs, GROUP:2 * GROUP]
        vv = qkv_ref[rs, 2 * GROUP:3 * GROUP]
        qn = qb * lax.rsqrt(seg_sum(qb * qb) + EPS) * (HEAD_DIM ** -0.5)
        kn = kb * lax.rsqrt(seg_sum(kb * kb) + EPS)
        g_parts = _split(g, 3)
        tri = cst["tri"][...]
        gc = _dot(tri, g_parts[0]) + _dot(tri, g_parts[1]) + _dot(tri, g_parts[2])
        gcr = _sel_dot(cst["ones_cc"][...], gc * cst["eye_w"][...], 3)
        dec = jnp.exp(gc - gcr)
        ratio_c = jnp.where(cst["causal_w"][...] > 0, dec, 0.0)
        ratio_s = jnp.where(cst["strict_w"][...] > 0, dec, 0.0)
        knb = _bf(kn)
        kbd = _blockdiag(knb, bd_std[...])
        lmat = beta * _dot_nt(knb, kbd) * ratio_s
        tinv = cst["eye_w"][...] - lmat * cst["lvl"][0]
        for li in range(1, len(LEVELS)):
            xb = wide_matmul(lmat * cst["lvl"][li], tinv, 3)
            tinv = tinv - wide_matmul(tinv, xb, 3)
        gam = jnp.exp(gc)
        u_v = wide_matmul(tinv, beta * vv, 3)
        w_k = wide_matmul(tinv, beta * gam * kn, 3)
        qk = _dot_nt(_bf(qn), kbd) * ratio_c
        s = sb_ref[...]
        sbf = _bf(s)
        u = u_v - _dot(_bf(w_k), sbf)
        ub = _bf(u)
        o = _dot(_bf(qn * gam), sbf) + _dot(_bf(qk), _blockdiag(ub, bd_std[...]))
        kt = kn * jnp.exp(gc[CHUNK - 1:CHUNK, :] - gc)
        ones = cst["ones_cw"][...]
        cd = jnp.exp(_dot_tn(g_parts[0], ones) + _dot_tn(g_parts[1], ones)
                     + _dot_tn(g_parts[2], ones))
        sb_ref[...] = s * cd + bd_std[...].astype(F32) * _dot_tn(_bf(kt), ub)
        o = head_norm(o) * par["gdn_nw"][...] * _silu(z_ref[rs, Z_BG:Z_BG + GROUP])
        o_ref[rs, GROUP:2 * GROUP] = _bf(o)

        logit = _dot_hp(sm, par["gla_up"][...]) + par["gla_bias"][...]
        logf_c = _log_sigmoid(logit) * (1.0 / GLA_GATE_NORM)
        o = gated_linear(z_ref[rs, Z_CQ:Z_CQ + GLA_KEYS] * (GLA_KEY_DIM ** -0.5),
                         z_ref[rs, Z_CK:Z_CK + GLA_KEYS],
                         z_ref[rs, Z_CV:Z_CV + GROUP], logf_c,
                         sc_ref, cst["bd_gla"], cst["s_gla"])
        o = head_norm(o) * par["gla_nw"][...] * _silu(z_ref[rs, Z_CG:Z_CG + GROUP])
        o_ref[rs, 2 * GROUP:3 * GROUP] = _bf(o)

        f_pre = z_ref[rs, Z_DF:Z_DF + GROUP]
        a = par["hg_loglb"][...]
        b = par["hg_log1mlb"][...] + _log_sigmoid(f_pre)
        logf_d = jnp.maximum(a, b) + jnp.log1p(jnp.exp(-jnp.abs(a - b)))
        k_d = par["hg_1mlb"][...] * jax.nn.sigmoid(-f_pre)
        o = gated_linear(z_ref[rs, Z_DQ:Z_DQ + GROUP], k_d,
                         z_ref[rs, Z_DI:Z_DI + GROUP], logf_d,
                         sd_ref, bd_std, bd_std)
        o = head_norm(o) * par["hg_nw"][...] * _silu(z_ref[rs, Z_DG:Z_DG + GROUP])
        o_ref[rs, 3 * GROUP:4 * GROUP] = _bf(o)
        return carry

    lax.fori_loop(0, n_chunks, chunk, 0)


def _mixer(z, cos_t, sin_t, consts, params, batch, tb):
    m = z.shape[0]
    t = m // batch
    nt = t // tb
    const_args = [consts[k] for k in _CONST_ORDER]
    param_args = [params[k] for k in _PARAM_ORDER]

    def full_spec(a):
        nd = a.ndim
        return pl.BlockSpec(a.shape, lambda b, i, _nd=nd: (0,) * _nd)

    row_map = lambda b, i: (b * nt + i, 0)
    body = functools.partial(_mixer_body, n_chunks=tb // CHUNK)
    return pl.pallas_call(
        body,
        out_shape=jax.ShapeDtypeStruct((m, N_HEADS * GROUP), BF16),
        grid=(batch, nt),
        in_specs=[pl.BlockSpec((tb, Z_WIDTH), row_map),
                  pl.BlockSpec((tb, LANES), row_map),
                  pl.BlockSpec((tb, LANES), row_map)]
                 + [full_spec(a) for a in const_args]
                 + [full_spec(a) for a in param_args],
        out_specs=pl.BlockSpec((tb, N_HEADS * GROUP), row_map),
        scratch_shapes=[
            pltpu.VMEM((GROUP, GROUP), F32),
            pltpu.VMEM((GROUP, GROUP), F32),
            pltpu.VMEM((GLA_KEYS, GROUP), F32),
            pltpu.VMEM((GROUP, GROUP), F32),
            pltpu.VMEM((8, 3 * GROUP), F32),
            pltpu.VMEM((tb, 3 * GROUP), F32),
        ],
        compiler_params=pltpu.CompilerParams(
            dimension_semantics=("parallel", "arbitrary"),
            vmem_limit_bytes=VMEM_LIMIT),
        name="mixer",
    )(z, cos_t, sin_t, *const_args, *param_args)


def _in_proj_perm():
    src = {}
    off = 0
    for name, width in (("rq", 256), ("rk", 256), ("rv", 256), ("rg", 256),
                        ("bq", 256), ("bk", 256), ("bv", 256), ("ba", 4), ("bb", 4), ("bg", 256),
                        ("cq", 128), ("ck", 128), ("cv", 256), ("clr", 16), ("cg", 256),
                        ("dq", 256), ("df", 256), ("di", 256), ("dg", 256)):
        src[name] = off
        off += width
    perm = np.full((Z_WIDTH,), -1, np.int64)
    j = np.arange(GROUP)
    rope = (j % LANES) // ROPE_HALF * HEAD_DIM + (j // LANES) * ROPE_HALF + j % ROPE_HALF
    perm[Z_RQ:Z_RQ + GROUP] = src["rq"] + rope
    perm[Z_RK:Z_RK + GROUP] = src["rk"] + rope
    for dst, name, width in ((Z_RV, "rv", 256), (Z_RG, "rg", 256), (Z_BQKV, "bq", 768),
                             (Z_BG, "bg", 256), (Z_CQ, "cq", 128), (Z_CK, "ck", 128),
                             (Z_CV, "cv", 256), (Z_CG, "cg", 256), (Z_DQ, "dq", 256),
                             (Z_DF, "df", 256), (Z_DI, "di", 256), (Z_DG, "dg", 256),
                             (Z_SMALL + SM_BA, "ba", 4), (Z_SMALL + SM_BB, "bb", 4),
                             (Z_SMALL + SM_LR, "clr", 16)):
        perm[dst:dst + width] = src[name] + np.arange(width)
    return perm


def kernel(x, mem, positions, mix_norm_w, w_in, gdn_conv_w, gdn_a_log, gdn_dt_bias, gdn_norm_w,
           gla_gk_up, gla_gk_bias, gla_norm_w, hgrn_lb_logits, hgrn_norm_w, w_out,
           xattn_norm_w, mem_norm_w, xattn_wq, xattn_wk, xattn_wv, xattn_wo,
           ffn_norm_w, ffn_up, ffn_conv_w, ffn_down, final_norm_w):
    batch, seq, d = x.shape
    depth = w_in.shape[0]
    m = batch * seq
    mem_len = mem.shape[1]

    np_consts = _mixer_constants()
    consts = {k: jnp.asarray(v, F32 if k in _F32_CONSTS else BF16) for k, v in np_consts.items()}

    perm = _in_proj_perm()
    pad = jnp.asarray(perm < 0)
    w_in_p = jnp.where(pad[None, None, :], 0.0, w_in[:, :, np.maximum(perm, 0)]).astype(BF16)

    freqs = ROPE_BASE ** (-jnp.arange(0, HEAD_DIM, 2, dtype=F32) / HEAD_DIM)
    freq_row = jnp.tile(freqs, N_HEADS).reshape(1, LANES)
    cos_t, sin_t = _rope_tables(positions.reshape(m, 1), freq_row, 1024)

    lb_all = jnp.cumsum(jax.nn.softmax(hgrn_lb_logits.astype(F32), axis=0), axis=0)
    lb_all = lb_all - lb_all[0]

    def head_row(v):
        return jnp.tile(v.astype(F32), N_HEADS).reshape(1, GROUP)

    h = x.reshape(m, d)
    mem2 = mem.reshape(batch * mem_len, d)
    for l in range(depth):
        gla_up = jnp.zeros((LANES, GLA_KEYS), F32).at[SM_LR:SM_LR + GLA_LOWRANK].set(gla_gk_up[l])
        params = {
            "gdn_conv": gdn_conv_w[l].astype(F32),
            "gdn_alog": jnp.repeat(gdn_a_log[l].astype(F32), HEAD_DIM).reshape(1, GROUP),
            "gdn_dt": jnp.repeat(gdn_dt_bias[l].astype(F32), HEAD_DIM).reshape(1, GROUP),
            "gdn_nw": head_row(gdn_norm_w[l]),
            "gla_up": gla_up,
            "gla_bias": gla_gk_bias[l].astype(F32).reshape(1, GLA_KEYS),
            "gla_nw": head_row(gla_norm_w[l]),
            "hg_loglb": jnp.log(lb_all[l]).reshape(1, GROUP),
            "hg_log1mlb": jnp.log1p(-lb_all[l]).reshape(1, GROUP),
            "hg_1mlb": (1.0 - lb_all[l]).reshape(1, GROUP),
            "hg_nw": head_row(hgrn_norm_w[l]),
        }
        z = _norm_matmul(h, mix_norm_w[l], w_in_p[l], F32, 512, 2048)
        o = _mixer(z, cos_t, sin_t, consts, params, batch, 256)
        h = _matmul_residual(o, w_out[l].astype(BF16), h, 1024)

        w_kv = jnp.concatenate([xattn_wk[l], xattn_wv[l]], axis=1).astype(BF16)
        kv = _norm_matmul(mem2, mem_norm_w, w_kv, BF16, 512, 2 * d)
        h = _xattn(h, xattn_norm_w[l], xattn_wq[l].astype(BF16),
                   kv.reshape(batch, mem_len, 2 * d), xattn_wo[l].astype(BF16), batch, 512)

        u = _norm_matmul(h, ffn_norm_w[l], ffn_up[l].astype(BF16), BF16, 512, 1408)
        h = _conv_ffn_down(u, ffn_conv_w[l].astype(F32), ffn_down[l].astype(BF16), h,
                           final_norm_w, seq, 256, l == depth - 1)
    return h.reshape(batch, seq, d)
```

```python
import functools

import numpy as np
import jax
import jax.numpy as jnp
from jax import lax
from jax.experimental import pallas as pl
from jax.experimental.pallas import tpu as pltpu

F32 = jnp.float32
BF16 = jnp.bfloat16

N_HEADS = 4
HEAD_DIM = 64
GROUP = N_HEADS * HEAD_DIM
GLA_KEY_DIM = 32
GLA_KEYS = N_HEADS * GLA_KEY_DIM
GLA_LOWRANK = 16
GLA_GATE_NORM = 16.0
GDN_CONV = 4
CHUNK = 64
ROPE_BASE = 10000.0
ROPE_HALF = HEAD_DIM // 2
XATTN_HEADS = 4
FFN_CONV = 3
EPS = 1e-6

LANES = 128
VMEM_LIMIT = 52 * 1024 * 1024

Z_RQ, Z_RK, Z_RV, Z_RG = 0, 256, 512, 768
Z_BQKV, Z_BG = 1024, 1792
Z_CQ, Z_CK, Z_CV, Z_CG = 2048, 2176, 2304, 2560
Z_DQ, Z_DF, Z_DI, Z_DG = 2816, 3072, 3328, 3584
Z_SMALL = 3840
Z_WIDTH = 4096
SM_BA, SM_BB, SM_LR = 0, 4, 8

LEVELS = (1, 2, 4, 8, 16, 32)
N_LEVELS = len(LEVELS)
GL_DIAG = N_LEVELS
GL_EDGE = N_LEVELS + 1


def _bf(x):
    return x.astype(BF16)


def _dot(a, b):
    return jnp.dot(a, b, preferred_element_type=F32)


def _dot_nt(a, b):
    return lax.dot_general(a, b, (((1,), (1,)), ((), ())), preferred_element_type=F32)


def _dot_tn(a, b):
    return lax.dot_general(a, b, (((0,), (0,)), ((), ())), preferred_element_type=F32)


def _split(x, n):
    parts = []
    r = x
    for i in range(n):
        p = r.astype(BF16)
        parts.append(p)
        if i < n - 1:
            r = r - p.astype(F32)
    return parts


def _sel_dot(c, x, n=2):
    acc = None
    for p in _split(x, n):
        t = _dot(c, p)
        acc = t if acc is None else acc + t
    return acc


def _dot_sel(x, c, n=2):
    acc = None
    for p in _split(x, n):
        t = _dot(p, c)
        acc = t if acc is None else acc + t
    return acc


def _dot_hp(a, b):
    a_hi, a_lo = _split(a, 2)
    b_hi, b_lo = _split(b, 2)
    return _dot(a_hi, b_hi) + _dot(a_lo, b_hi) + _dot(a_hi, b_lo)


def _silu(x):
    return x * jax.nn.sigmoid(x)


def _log_sigmoid(x):
    return jnp.minimum(x, 0.0) - jnp.log1p(jnp.exp(-jnp.abs(x)))


def _softplus(x):
    return jnp.maximum(x, 0.0) + jnp.log1p(jnp.exp(-jnp.abs(x)))


def _rms(x, w):
    ms = jnp.mean(x * x, axis=-1, keepdims=True)
    return x * lax.rsqrt(ms + EPS) * w


def _blockdiag(x, mask):
    return jnp.concatenate([x, x, x, x], axis=0) * mask


def _run_interleaved(gens):
    gens = list(gens)
    while gens:
        alive = []
        for g in gens:
            try:
                next(g)
                alive.append(g)
            except StopIteration:
                pass
        gens = alive


def _norm_matmul_body(x_ref, nw_ref, w_ref, o_ref, xn_ref):
    @pl.when(pl.program_id(1) == 0)
    def _():
        xn_ref[...] = _bf(_rms(x_ref[...], nw_ref[...]))

    o_ref[...] = _dot(xn_ref[...], w_ref[...]).astype(o_ref.dtype)


def _norm_matmul(x, nw, w, out_dtype, tm, tn):
    m, d = x.shape
    n = w.shape[1]
    return pl.pallas_call(
        _norm_matmul_body,
        out_shape=jax.ShapeDtypeStruct((m, n), out_dtype),
        grid=(m // tm, n // tn),
        in_specs=[
            pl.BlockSpec((tm, d), lambda i, j: (i, 0)),
            pl.BlockSpec((1, d), lambda i, j: (0, 0)),
            pl.BlockSpec((d, tn), lambda i, j: (0, j)),
        ],
        out_specs=pl.BlockSpec((tm, tn), lambda i, j: (i, j)),
        scratch_shapes=[pltpu.VMEM((tm, d), BF16)],
        compiler_params=pltpu.CompilerParams(
            dimension_semantics=("parallel", "arbitrary"),
            vmem_limit_bytes=VMEM_LIMIT),
        name="norm_matmul",
    )(x, nw.reshape(1, d), w)


def _matmul_residual_body(a_ref, w_ref, r_ref, o_ref):
    o_ref[...] = r_ref[...] + _dot(a_ref[...], w_ref[...])


def _matmul_residual(a, w, res, tm):
    m, k = a.shape
    n = w.shape[1]
    return pl.pallas_call(
        _matmul_residual_body,
        out_shape=jax.ShapeDtypeStruct((m, n), F32),
        grid=(m // tm,),
        in_specs=[
            pl.BlockSpec((tm, k), lambda i: (i, 0)),
            pl.BlockSpec((k, n), lambda i: (0, 0)),
            pl.BlockSpec((tm, n), lambda i: (i, 0)),
        ],
        out_specs=pl.BlockSpec((tm, n), lambda i: (i, 0)),
        compiler_params=pltpu.CompilerParams(
            dimension_semantics=("parallel",),
            vmem_limit_bytes=VMEM_LIMIT),
        name="matmul_residual",
    )(a, w, res)


def _rope_table_body(pos_ref, freq_ref, cos_ref, sin_ref):
    ang = pos_ref[...].astype(F32) * freq_ref[...]
    cos_ref[...] = jnp.cos(ang)
    sin_ref[...] = jnp.sin(ang)


def _rope_tables(pos, freq_row, tm):
    m = pos.shape[0]
    return pl.pallas_call(
        _rope_table_body,
        out_shape=(jax.ShapeDtypeStruct((m, LANES), F32),
                   jax.ShapeDtypeStruct((m, LANES), F32)),
        grid=(m // tm,),
        in_specs=[
            pl.BlockSpec((tm, 1), lambda i: (i, 0)),
            pl.BlockSpec((1, LANES), lambda i: (0, 0)),
        ],
        out_specs=(pl.BlockSpec((tm, LANES), lambda i: (i, 0)),
                   pl.BlockSpec((tm, LANES), lambda i: (i, 0))),
        compiler_params=pltpu.CompilerParams(dimension_semantics=("parallel",)),
        name="rope_tables",
    )(pos, freq_row)


def _xattn_body(h_ref, nw_ref, wq_ref, k_ref, v_ref, wo_ref, o_ref, att_ref):
    h = h_ref[...]
    d = h.shape[-1]
    hd = d // XATTN_HEADS
    hn = _bf(_rms(h, nw_ref[...]))
    q = _bf(_dot(hn, wq_ref[...]) * (hd ** -0.5))
    for i in range(XATTN_HEADS):
        cs = slice(i * hd, (i + 1) * hd)
        s = _dot_nt(q[:, cs], k_ref[0, :, cs])
        s = s - jnp.max(s, axis=-1, keepdims=True)
        p = jnp.exp(s)
        p = p / jnp.sum(p, axis=-1, keepdims=True)
        att_ref[:, cs] = _bf(_dot(_bf(p), v_ref[0, :, cs]))
    o_ref[...] = h + _dot(att_ref[...], wo_ref[...])


def _xattn(h, nw, wq, kv, wo, batch, tq):
    m, d = h.shape
    t = m // batch
    mem_len = kv.shape[1]
    nt = t // tq
    return pl.pallas_call(
        _xattn_body,
        out_shape=jax.ShapeDtypeStruct((m, d), F32),
        grid=(batch, nt),
        in_specs=[
            pl.BlockSpec((tq, d), lambda b, i: (b * nt + i, 0)),
            pl.BlockSpec((1, d), lambda b, i: (0, 0)),
            pl.BlockSpec((d, d), lambda b, i: (0, 0)),
            pl.BlockSpec((1, mem_len, d), lambda b, i: (b, 0, 0)),
            pl.BlockSpec((1, mem_len, d), lambda b, i: (b, 0, 1)),
            pl.BlockSpec((d, d), lambda b, i: (0, 0)),
        ],
        out_specs=pl.BlockSpec((tq, d), lambda b, i: (b * nt + i, 0)),
        scratch_shapes=[pltpu.VMEM((tq, d), BF16)],
        compiler_params=pltpu.CompilerParams(
            dimension_semantics=("parallel", "parallel"),
            vmem_limit_bytes=VMEM_LIMIT),
        name="xattn",
    )(h, nw.reshape(1, d), wq, kv, kv, wo)


def _shift_rows(x, k, tail, row):
    y = pltpu.roll(x, k, 0)
    for j in range(k):
        src = tail.shape[0] - k + j
        y = jnp.where(row == j, tail[src:src + 1, :], y)
    return y


def _conv_ffn_down_body(u_ref, halo_ref, cw_ref, wd_ref, r_ref, fw_ref, o_ref, act_ref,
                        *, tiles_per_seq, col_chunk, final_norm):
    tm = u_ref.shape[0]
    d_ff = act_ref.shape[1]
    first = (pl.program_id(0) % tiles_per_seq) == 0
    keep = jnp.where(first, 0.0, 1.0)
    row = lax.broadcasted_iota(jnp.int32, (tm, 1), 0)
    hrows = halo_ref.shape[0]

    def conv(c0):
        cs = slice(c0, c0 + col_chunk)
        x = u_ref[:, cs].astype(F32)
        tail = halo_ref[hrows - 8:hrows, cs].astype(F32) * keep
        w = cw_ref[:, cs]
        y = x * w[2:3, :]
        y = y + _shift_rows(x, 1, tail, row) * w[1:2, :]
        y = y + _shift_rows(x, 2, tail, row) * w[0:1, :]
        return y

    for c0 in range(0, d_ff, col_chunk):
        gate = conv(c0)
        val = conv(d_ff + c0)
        act_ref[:, c0:c0 + col_chunk] = _bf(_silu(gate) * val)
    out = r_ref[...] + _dot(act_ref[...], wd_ref[...])
    if final_norm:
        out = _rms(out, fw_ref[...])
    o_ref[...] = out


def _conv_ffn_down(u, conv_w, w_down, res, final_w, seq_len, tm, final_norm):
    m, n2 = u.shape
    d_ff = n2 // 2
    d = w_down.shape[1]
    halo = 16
    col_chunk = 256
    body = functools.partial(_conv_ffn_down_body, tiles_per_seq=seq_len // tm,
                             col_chunk=col_chunk, final_norm=final_norm)
    return pl.pallas_call(
        body,
        out_shape=jax.ShapeDtypeStruct((m, d), F32),
        grid=(m // tm,),
        in_specs=[
            pl.BlockSpec((tm, n2), lambda i: (i, 0)),
            pl.BlockSpec((halo, n2), lambda i: (jnp.maximum(i * (tm // halo) - 1, 0), 0)),
            pl.BlockSpec((FFN_CONV, n2), lambda i: (0, 0)),
            pl.BlockSpec((d_ff, d), lambda i: (0, 0)),
            pl.BlockSpec((tm, d), lambda i: (i, 0)),
            pl.BlockSpec((1, d), lambda i: (0, 0)),
        ],
        out_specs=pl.BlockSpec((tm, d), lambda i: (i, 0)),
        scratch_shapes=[pltpu.VMEM((tm, d_ff), BF16)],
        compiler_params=pltpu.CompilerParams(
            dimension_semantics=("parallel",),
            vmem_limit_bytes=VMEM_LIMIT),
        name="conv_ffn_down",
    )(u, u, conv_w, w_down, res, final_w.reshape(1, d))


def _mixer_constants(tb):
    c = CHUNK
    ar = np.arange
    std_head = ar(GROUP) // HEAD_DIM
    rope_head = (ar(GROUP) % LANES) // ROPE_HALF
    gla_head = ar(GLA_KEYS) // GLA_KEY_DIM
    row_head = ar(GROUP) // c
    s_of = ar(GROUP) % c
    t = ar(c)
    tt = ar(tb) % c
    same_chunk = (ar(tb)[:, None] // c) == (ar(tb)[None, :] // c)

    out = {}
    bd_std = row_head[:, None] == std_head[None, :]
    out["bd_std"] = (bd_std, BF16)
    out["bd_std2"] = (np.concatenate([bd_std, bd_std], axis=1), BF16)
    out["bd_rope"] = (row_head[:, None] == rope_head[None, :], BF16)
    out["bd_gla"] = (row_head[:, None] == gla_head[None, :], BF16)

    log_gamma = np.log(1.0 - np.exp2(-5.0 - ar(N_HEADS, dtype=np.float64)))
    scale = HEAD_DIM ** -0.5
    rel = t[:, None] - s_of[None, :]
    lg_w = log_gamma[std_head][None, :]
    out["ret_d"] = (np.where(rel >= 0, np.exp(lg_w * np.maximum(rel, 0)), 0.0) * scale, F32)
    lg_r = log_gamma[rope_head][None, :]
    out["ret_qdec"] = (np.exp(lg_r * (tt[:, None] + 1.0)), F32)
    out["ret_kdec"] = (np.exp(lg_r * (c - 1.0 - tt[:, None])) * scale, F32)
    out["ret_cdec"] = (np.exp(log_gamma[std_head] * c)[None, :], F32)

    lvl, sel_q, sel_k = [], [], []
    for b in LEVELS:
        blk = t // (2 * b)
        second = (t % (2 * b)) >= b
        bound = blk * 2 * b + b - 1
        msk = (blk[:, None] == blk[None, :]) & second[:, None] & (~second)[None, :]
        lvl.append(msk[:, s_of])
        sel_q.append(second[:, None] & (t[None, :] > bound[:, None]) & (t[None, :] <= t[:, None]))
        sel_k.append((~second)[:, None] & (t[None, :] > t[:, None]) & (t[None, :] <= bound[:, None]))
    lvl.append(t[:, None] == s_of[None, :])
    out["lvl"] = (np.stack(lvl), F32)
    tri = t[None, :] <= t[:, None]
    tail = t[None, :] > t[:, None]
    eye_b = np.eye(tb // c, dtype=bool)
    sel = [np.kron(eye_b, s_) for s_ in sel_q + sel_k + [tri, tail]]
    out["sel"] = (np.stack(sel), BF16)
    out["ones_b"] = (same_chunk, BF16)
    out["trit_w"] = (tt[:, None] <= s_of[None, :], F32)
    out["causal_w"] = (tt[:, None] >= s_of[None, :], F32)
    out["strict_w"] = (tt[:, None] > s_of[None, :], F32)
    out["eye_w"] = (t[:, None] == s_of[None, :], F32)
    eab = np.zeros((LANES, 2 * GROUP))
    eab[SM_BA + std_head, ar(GROUP)] = 1.0
    eab[SM_BB + std_head, GROUP + ar(GROUP)] = 1.0
    out["exp_ab"] = (eab, BF16)
    return out


_CONST_ORDER = ("bd_std", "bd_std2", "bd_rope", "bd_gla", "ret_d", "ret_qdec", "ret_kdec",
                "ret_cdec", "lvl", "sel", "ones_b", "trit_w", "causal_w", "strict_w", "eye_w",
                "exp_ab")
_PARAM_ORDER = ("gdn_conv", "gdn_alog", "gdn_dt", "gdn_nw", "gla_up", "gla_bias", "gla_nw",
                "hg_loglb", "hg_log1mlb", "hg_1mlb", "hg_nw")
_SCRATCH_ORDER = ("sa", "sb", "sct", "sdt", "tail",
                  "aq", "aqd", "ak", "akd", "av",
                  "bq", "bk", "bbr", "brc", "brhs", "bqg", "bkt", "bcd",
                  "cqx", "ckx", "cv", "ccd", "dqx", "dkx", "dv", "dcd", "oscr")


def _mixer_scratch(tb):
    f, b = F32, BF16
    shapes = {
        "sa": ((GROUP, GROUP), f), "sb": ((GROUP, GROUP), f),
        "sct": ((GROUP, GLA_KEYS), f), "sdt": ((GROUP, GROUP), f),
        "tail": ((8, 3 * GROUP), f),
        "aq": ((tb, GROUP), b), "aqd": ((tb, GROUP), b), "ak": ((tb, GROUP), b),
        "akd": ((tb, GROUP), b), "av": ((tb, GROUP), b),
        "bq": ((tb, GROUP), b), "bk": ((tb, GROUP), b), "bbr": ((tb, GROUP), f),
        "brc": ((tb, GROUP), f), "brhs": ((tb, 2 * GROUP), b), "bqg": ((tb, GROUP), b),
        "bkt": ((tb, GROUP), b), "bcd": ((tb, GROUP), f),
        "cqx": ((N_LEVELS + 2, tb, GLA_KEYS), b), "ckx": ((N_LEVELS + 2, tb, GLA_KEYS), b),
        "cv": ((tb, GROUP), b), "ccd": ((tb, GLA_KEYS), f),
        "dqx": ((N_LEVELS + 2, tb, GROUP), b), "dkx": ((N_LEVELS + 2, tb, GROUP), b),
        "dv": ((tb, GROUP), b), "dcd": ((tb, GROUP), f),
        "oscr": ((tb, N_HEADS * GROUP), f),
    }
    return [pltpu.VMEM(*shapes[k]) for k in _SCRATCH_ORDER]


def _mixer_body(*refs, n_chunks):
    z_ref, cos_ref, sin_ref = refs[:3]
    nc, npar = len(_CONST_ORDER), len(_PARAM_ORDER)
    cst = dict(zip(_CONST_ORDER, refs[3:3 + nc]))
    par = dict(zip(_PARAM_ORDER, refs[3 + nc:3 + nc + npar]))
    o_ref = refs[3 + nc + npar]
    scr = dict(zip(_SCRATCH_ORDER, refs[4 + nc + npar:]))
    tb = z_ref.shape[0]
    rows = [slice(n * CHUNK, (n + 1) * CHUNK) for n in range(n_chunks)]
    bd_std = cst["bd_std"]

    @pl.when(pl.program_id(1) == 0)
    def _():
        for k in ("sa", "sb", "sct", "sdt", "tail"):
            scr[k][...] = jnp.zeros_like(scr[k])

    def seg_sum(x):
        return _dot_sel(x, bd_std[...])

    cos = cos_ref[...]
    sin = sin_ref[...]

    def rope(col):
        x1 = z_ref[:, col:col + LANES]
        x2 = z_ref[:, col + LANES:col + 2 * LANES]
        return jnp.concatenate([x1 * cos - x2 * sin, x1 * sin + x2 * cos], axis=1)

    qr = rope(Z_RQ)
    scr["aq"][...] = _bf(qr)
    scr["aqd"][...] = _bf(qr * cst["ret_qdec"][...])
    kr = rope(Z_RK)
    scr["ak"][...] = _bf(kr)
    scr["akd"][...] = _bf(kr * cst["ret_kdec"][...])
    scr["av"][...] = _bf(z_ref[:, Z_RV:Z_RV + GROUP])

    row = lax.broadcasted_iota(jnp.int32, (tb, 1), 0)
    x = z_ref[:, Z_BQKV:Z_BQKV + 3 * GROUP]
    tail = scr["tail"][...]
    cw = par["gdn_conv"][...]
    y = x * cw[GDN_CONV - 1:GDN_CONV, :]
    for k in range(1, GDN_CONV):
        y = y + _shift_rows(x, k, tail, row) * cw[GDN_CONV - 1 - k:GDN_CONV - k, :]
    scr["tail"][...] = x[tb - 8:tb, :]
    qkv = _silu(y)
    qb, kb, vv = qkv[:, 0:GROUP], qkv[:, GROUP:2 * GROUP], qkv[:, 2 * GROUP:3 * GROUP]
    sm = z_ref[:, Z_SMALL:Z_SMALL + LANES]
    bab = _dot_sel(sm, cst["exp_ab"][...])
    g = -jnp.exp(par["gdn_alog"][...]) * _softplus(bab[:, 0:GROUP] + par["gdn_dt"][...])
    beta = jax.nn.sigmoid(bab[:, GROUP:2 * GROUP])
    qn = qb * lax.rsqrt(seg_sum(qb * qb) + EPS) * (HEAD_DIM ** -0.5)
    kn = kb * lax.rsqrt(seg_sum(kb * kb) + EPS)
    ones_b = cst["ones_b"][...]
    gc = _sel_dot(cst["sel"][2 * N_LEVELS], g)
    gcr = _sel_dot(ones_b, g * cst["trit_w"][...])
    glast = _sel_dot(ones_b, g)
    dec = jnp.exp(gc - gcr)
    gam = jnp.exp(gc)
    scr["bq"][...] = _bf(qn)
    scr["bk"][...] = _bf(kn)
    scr["bbr"][...] = beta * jnp.where(cst["strict_w"][...] > 0, dec, 0.0)
    scr["brc"][...] = jnp.where(cst["causal_w"][...] > 0, dec, 0.0)
    scr["brhs"][...] = _bf(jnp.concatenate([beta * vv, beta * gam * kn], axis=1))
    scr["bqg"][...] = _bf(qn * gam)
    scr["bkt"][...] = _bf(kn * jnp.exp(glast - gc))
    scr["bcd"][...] = jnp.exp(glast)

    def gl_prep(q, k, v, logf, qx_ref, kx_ref, v_ref, cd_ref):
        hi, lo = _split(logf, 2)
        for j in range(2 * N_LEVELS + 2):
            sel = cst["sel"][j]
            e = jnp.exp(_dot(sel, hi) + _dot(sel, lo))
            if j < N_LEVELS:
                qx_ref[j] = _bf(q * e)
            elif j < 2 * N_LEVELS:
                kx_ref[j - N_LEVELS] = _bf(k * e)
            elif j == 2 * N_LEVELS:
                qx_ref[GL_EDGE] = _bf(q * e)
                cd_ref[...] = e
            else:
                kx_ref[GL_EDGE] = _bf(k * e)
        qx_ref[GL_DIAG] = _bf(q)
        kx_ref[GL_DIAG] = _bf(k)
        v_ref[...] = _bf(v)

    logit = _dot_hp(sm, par["gla_up"][...]) + par["gla_bias"][...]
    gl_prep(z_ref[:, Z_CQ:Z_CQ + GLA_KEYS] * (GLA_KEY_DIM ** -0.5),
            z_ref[:, Z_CK:Z_CK + GLA_KEYS], z_ref[:, Z_CV:Z_CV + GROUP],
            _log_sigmoid(logit) * (1.0 / GLA_GATE_NORM),
            scr["cqx"], scr["ckx"], scr["cv"], scr["ccd"])

    f_pre = z_ref[:, Z_DF:Z_DF + GROUP]
    a = par["hg_loglb"][...]
    b = par["hg_log1mlb"][...] + _log_sigmoid(f_pre)
    logf_d = jnp.maximum(a, b) + jnp.log1p(jnp.exp(-jnp.abs(a - b)))
    gl_prep(z_ref[:, Z_DQ:Z_DQ + GROUP], par["hg_1mlb"][...] * jax.nn.sigmoid(-f_pre),
            z_ref[:, Z_DI:Z_DI + GROUP], logf_d,
            scr["dqx"], scr["dkx"], scr["dv"], scr["dcd"])

    pa, pb, pc, pd = {}, {}, {}, {}

    def ret_p1(n):
        rs = rows[n]
        p = _dot_nt(scr["aq"][rs, :], _blockdiag(scr["ak"][rs, :], cst["bd_rope"][...]))
        yield
        pa[n] = _bf(p * cst["ret_d"][...])

    def gl_p1(n, qx_ref, kx_ref, bd_k, store):
        rs = rows[n]
        p = None
        for l in range(N_LEVELS + 1):
            t = _dot_nt(qx_ref[l, rs, :], _blockdiag(kx_ref[l, rs, :], bd_k[...]))
            yield
            t = t * cst["lvl"][l]
            p = t if p is None else p + t
        store[n] = _bf(p)

    def wide_matmul(a, b):
        return _dot(_bf(a), _blockdiag(_bf(b), bd_std[...]))

    def gdn_p1(n):
        rs = rows[n]
        knb = scr["bk"][rs, :]
        kbd = _blockdiag(knb, bd_std[...])
        kk = _dot_nt(knb, kbd)
        qk = _dot_nt(scr["bq"][rs, :], kbd)
        yield
        lmat = scr["bbr"][rs, :] * kk
        tinv = cst["eye_w"][...] - lmat * cst["lvl"][0]
        for li in range(1, N_LEVELS):
            xb = wide_matmul(lmat * cst["lvl"][li], tinv)
            yield
            tinv = tinv - wide_matmul(tinv, xb)
            yield
        uw = _dot(_bf(tinv), _blockdiag(scr["brhs"][rs, :], cst["bd_std2"][...]))
        yield
        pb[n] = (uw[:, 0:GROUP], _bf(uw[:, GROUP:2 * GROUP]), _bf(qk * scr["brc"][rs, :]))

    gens = []
    for n in range(n_chunks):
        gens += [gdn_p1(n), gl_p1(n, scr["cqx"], scr["ckx"], cst["bd_gla"], pc),
                 gl_p1(n, scr["dqx"], scr["dkx"], bd_std, pd), ret_p1(n)]
    _run_interleaved(gens)

    oscr = scr["oscr"]

    def ret_p2():
        s = scr["sa"][...]
        for n, rs in enumerate(rows):
            vbd = _blockdiag(scr["av"][rs, :], bd_std[...])
            o = _dot(pa[n], vbd) + _dot(scr["aqd"][rs, :], _bf(s))
            kv = _dot_tn(_blockdiag(scr["akd"][rs, :], cst["bd_rope"][...]), vbd)
            yield
            s = s * cst["ret_cdec"][...] + kv
            oscr[rs, 0:GROUP] = o
        scr["sa"][...] = s

    def gdn_p2():
        s = scr["sb"][...]
        for n, rs in enumerate(rows):
            u_v, w_k, qk = pb[n]
            r = _dot(jnp.concatenate([w_k, scr["bqg"][rs, :]], axis=0), _bf(s))
            yield
            u = u_v - r[0:CHUNK]
            ubd = _blockdiag(_bf(u), bd_std[...])
            o = r[CHUNK:2 * CHUNK] + _dot(qk, ubd)
            kv = _dot_tn(_blockdiag(scr["bkt"][rs, :], bd_std[...]), ubd)
            yield
            s = s * scr["bcd"][rs.start:rs.start + 1, :] + kv
            oscr[rs, GROUP:2 * GROUP] = o
        scr["sb"][...] = s

    def gl_p2(store, qx_ref, kx_ref, v_ref, cd_ref, bd_k, st_ref, col):
        st = st_ref[...]
        for n, rs in enumerate(rows):
            vbd = _blockdiag(v_ref[rs, :], bd_std[...])
            o = _dot(store[n], vbd) + _dot_nt(qx_ref[GL_EDGE, rs, :], _bf(st))
            kv = _dot_tn(vbd, _blockdiag(kx_ref[GL_EDGE, rs, :], bd_k[...]))
            yield
            st = st * cd_ref[rs.stop - 1:rs.stop, :] + kv
            oscr[rs, col:col + GROUP] = o
        st_ref[...] = st

    _run_interleaved([
        gdn_p2(), ret_p2(),
        gl_p2(pc, scr["cqx"], scr["ckx"], scr["cv"], scr["ccd"], cst["bd_gla"], scr["sct"],
              2 * GROUP),
        gl_p2(pd, scr["dqx"], scr["dkx"], scr["dv"], scr["dcd"], bd_std, scr["sdt"],
              3 * GROUP)])

    inv_hd = 1.0 / HEAD_DIM
    gains = (None, par["gdn_nw"], par["gla_nw"], par["hg_nw"])
    gates = (Z_RG, Z_BG, Z_CG, Z_DG)
    for i in range(4):
        cs = slice(i * GROUP, (i + 1) * GROUP)
        o = oscr[:, cs]
        o = o * lax.rsqrt(seg_sum(o * o) * inv_hd + EPS)
        if gains[i] is not None:
            o = o * gains[i][...]
        o_ref[:, cs] = _bf(o * _silu(z_ref[:, gates[i]:gates[i] + GROUP]))


def _mixer(z, cos_t, sin_t, params, batch, tb):
    m = z.shape[0]
    t = m // batch
    nt = t // tb
    consts = {k: jnp.asarray(v, dt) for k, (v, dt) in _mixer_constants(tb).items()}
    const_args = [consts[k] for k in _CONST_ORDER]
    param_args = [params[k] for k in _PARAM_ORDER]

    def full_spec(a):
        nd = a.ndim
        return pl.BlockSpec(a.shape, lambda b, i, _nd=nd: (0,) * _nd)

    row_map = lambda b, i: (b * nt + i, 0)
    body = functools.partial(_mixer_body, n_chunks=tb // CHUNK)
    return pl.pallas_call(
        body,
        out_shape=jax.ShapeDtypeStruct((m, N_HEADS * GROUP), BF16),
        grid=(batch, nt),
        in_specs=[pl.BlockSpec((tb, Z_WIDTH), row_map),
                  pl.BlockSpec((tb, LANES), row_map),
                  pl.BlockSpec((tb, LANES), row_map)]
                 + [full_spec(a) for a in const_args]
                 + [full_spec(a) for a in param_args],
        out_specs=pl.BlockSpec((tb, N_HEADS * GROUP), row_map),
        scratch_shapes=_mixer_scratch(tb),
        compiler_params=pltpu.CompilerParams(
            dimension_semantics=("parallel", "arbitrary"),
            vmem_limit_bytes=VMEM_LIMIT),
        name="mixer",
    )(z, cos_t, sin_t, *const_args, *param_args)


def _in_proj_perm():
    src = {}
    off = 0
    for name, width in (("rq", 256), ("rk", 256), ("rv", 256), ("rg", 256),
                        ("bq", 256), ("bk", 256), ("bv", 256), ("ba", 4), ("bb", 4), ("bg", 256),
                        ("cq", 128), ("ck", 128), ("cv", 256), ("clr", 16), ("cg", 256),
                        ("dq", 256), ("df", 256), ("di", 256), ("dg", 256)):
        src[name] = off
        off += width
    perm = np.full((Z_WIDTH,), -1, np.int64)
    j = np.arange(GROUP)
    rope = (j % LANES) // ROPE_HALF * HEAD_DIM + (j // LANES) * ROPE_HALF + j % ROPE_HALF
    perm[Z_RQ:Z_RQ + GROUP] = src["rq"] + rope
    perm[Z_RK:Z_RK + GROUP] = src["rk"] + rope
    for dst, name, width in ((Z_RV, "rv", 256), (Z_RG, "rg", 256), (Z_BQKV, "bq", 768),
                             (Z_BG, "bg", 256), (Z_CQ, "cq", 128), (Z_CK, "ck", 128),
                             (Z_CV, "cv", 256), (Z_CG, "cg", 256), (Z_DQ, "dq", 256),
                             (Z_DF, "df", 256), (Z_DI, "di", 256), (Z_DG, "dg", 256),
                             (Z_SMALL + SM_BA, "ba", 4), (Z_SMALL + SM_BB, "bb", 4),
                             (Z_SMALL + SM_LR, "clr", 16)):
        perm[dst:dst + width] = src[name] + np.arange(width)
    return perm


def kernel(x, mem, positions, mix_norm_w, w_in, gdn_conv_w, gdn_a_log, gdn_dt_bias, gdn_norm_w,
           gla_gk_up, gla_gk_bias, gla_norm_w, hgrn_lb_logits, hgrn_norm_w, w_out,
           xattn_norm_w, mem_norm_w, xattn_wq, xattn_wk, xattn_wv, xattn_wo,
           ffn_norm_w, ffn_up, ffn_conv_w, ffn_down, final_norm_w):
    batch, seq, d = x.shape
    depth = w_in.shape[0]
    m = batch * seq
    mem_len = mem.shape[1]

    perm = _in_proj_perm()
    pad = jnp.asarray(perm < 0)
    w_in_p = jnp.where(pad[None, None, :], 0.0, w_in[:, :, np.maximum(perm, 0)]).astype(BF16)

    freqs = ROPE_BASE ** (-jnp.arange(0, HEAD_DIM, 2, dtype=F32) / HEAD_DIM)
    freq_row = jnp.tile(freqs, N_HEADS).reshape(1, LANES)
    cos_t, sin_t = _rope_tables(positions.reshape(m, 1), freq_row, 1024)

    lb_all = jnp.cumsum(jax.nn.softmax(hgrn_lb_logits.astype(F32), axis=0), axis=0)
    lb_all = lb_all - lb_all[0]

    def head_row(v):
        return jnp.tile(v.astype(F32), N_HEADS).reshape(1, GROUP)

    h = x.reshape(m, d)
    mem2 = mem.reshape(batch * mem_len, d)
    for l in range(depth):
        gla_up = jnp.zeros((LANES, GLA_KEYS), F32).at[SM_LR:SM_LR + GLA_LOWRANK].set(gla_gk_up[l])
        params = {
            "gdn_conv": gdn_conv_w[l].astype(F32),
            "gdn_alog": jnp.repeat(gdn_a_log[l].astype(F32), HEAD_DIM).reshape(1, GROUP),
            "gdn_dt": jnp.repeat(gdn_dt_bias[l].astype(F32), HEAD_DIM).reshape(1, GROUP),
            "gdn_nw": head_row(gdn_norm_w[l]),
            "gla_up": gla_up,
            "gla_bias": gla_gk_bias[l].astype(F32).reshape(1, GLA_KEYS),
            "gla_nw": head_row(gla_norm_w[l]),
            "hg_loglb": jnp.log(lb_all[l]).reshape(1, GROUP),
            "hg_log1mlb": jnp.log1p(-lb_all[l]).reshape(1, GROUP),
            "hg_1mlb": (1.0 - lb_all[l]).reshape(1, GROUP),
            "hg_nw": head_row(hgrn_norm_w[l]),
        }
        z = _norm_matmul(h, mix_norm_w[l], w_in_p[l], F32, 512, 2048)
        o = _mixer(z, cos_t, sin_t, params, batch, 256)
        h = _matmul_residual(o, w_out[l].astype(BF16), h, 1024)

        w_kv = jnp.concatenate([xattn_wk[l], xattn_wv[l]], axis=1).astype(BF16)
        kv = _norm_matmul(mem2, mem_norm_w, w_kv, BF16, 512, 2 * d)
        h = _xattn(h, xattn_norm_w[l], xattn_wq[l].astype(BF16),
                   kv.reshape(batch, mem_len, 2 * d), xattn_wo[l].astype(BF16), batch, 512)

        u = _norm_matmul(h, ffn_norm_w[l], ffn_up[l].astype(BF16), BF16, 512, 1408)
        h = _conv_ffn_down(u, ffn_conv_w[l].astype(F32), ffn_down[l].astype(BF16), h,
                           final_norm_w, seq, 256, l == depth - 1)
    return h.reshape(batch, seq, d)
```

```python
import functools

import numpy as np
import jax
import jax.numpy as jnp
from jax import lax
from jax.experimental import pallas as pl
from jax.experimental.pallas import tpu as pltpu

F32 = jnp.float32
BF16 = jnp.bfloat16

N_HEADS = 4
HEAD_DIM = 64
GROUP = N_HEADS * HEAD_DIM
GLA_KEY_DIM = 32
GLA_KEYS = N_HEADS * GLA_KEY_DIM
GLA_LOWRANK = 16
GLA_GATE_NORM = 16.0
GDN_CONV = 4
CHUNK = 64
ROPE_BASE = 10000.0
ROPE_HALF = HEAD_DIM // 2
XATTN_HEADS = 4
FFN_CONV = 3
EPS = 1e-6

LANES = 128
VMEM_LIMIT = 52 * 1024 * 1024

Z_RQ, Z_RK, Z_RV, Z_RG = 0, 256, 512, 768
Z_BQKV, Z_BG = 1024, 1792
Z_CQ, Z_CK, Z_CV, Z_CG = 2048, 2176, 2304, 2560
Z_DQ, Z_DI, Z_DG = 2816, 3072, 3328
Z_WIDTH = 3584
F_DF, F_SMALL = 0, 256
F_WIDTH = 384
SM_BA, SM_BB, SM_LR = 0, 4, 8

LEVELS = (1, 2, 4, 8, 16, 32)
N_LEVELS = len(LEVELS)
GL_DIAG = N_LEVELS
GL_EDGE = N_LEVELS + 1


def _bf(x):
    return x.astype(BF16)


def _dot(a, b):
    return jnp.dot(a, b, preferred_element_type=F32)


def _dot_nt(a, b):
    return lax.dot_general(a, b, (((1,), (1,)), ((), ())), preferred_element_type=F32)


def _dot_tn(a, b):
    return lax.dot_general(a, b, (((0,), (0,)), ((), ())), preferred_element_type=F32)


def _split(x, n):
    parts = []
    r = x
    for i in range(n):
        p = r.astype(BF16)
        parts.append(p)
        if i < n - 1:
            r = r - p.astype(F32)
    return parts


def _sel_dot(c, x, n=2):
    acc = None
    for p in _split(x, n):
        t = _dot(c, p)
        acc = t if acc is None else acc + t
    return acc


def _dot_sel(x, c, n=2):
    acc = None
    for p in _split(x, n):
        t = _dot(p, c)
        acc = t if acc is None else acc + t
    return acc


def _dot_hp(a, b):
    a_hi, a_lo = _split(a, 2)
    b_hi, b_lo = _split(b, 2)
    return _dot(a_hi, b_hi) + _dot(a_lo, b_hi) + _dot(a_hi, b_lo)


def _silu(x):
    return x * jax.nn.sigmoid(x)


def _log_sigmoid(x):
    return jnp.minimum(x, 0.0) - jnp.log1p(jnp.exp(-jnp.abs(x)))


def _softplus(x):
    return jnp.maximum(x, 0.0) + jnp.log1p(jnp.exp(-jnp.abs(x)))


def _rms(x, w):
    ms = jnp.mean(x * x, axis=-1, keepdims=True)
    return x * lax.rsqrt(ms + EPS) * w


def _blockdiag(x, mask):
    return jnp.concatenate([x, x, x, x], axis=0) * mask


def _run_interleaved(gens):
    gens = list(gens)
    while gens:
        alive = []
        for g in gens:
            try:
                next(g)
                alive.append(g)
            except StopIteration:
                pass
        gens = alive


def _col_chunks(n, width):
    return [(c0, min(width, n - c0)) for c0 in range(0, n, width)]


def _norm_matmul_body(x_ref, nw_ref, w_ref, *o_refs, widths, col_chunk):
    xn = _bf(_rms(x_ref[...], nw_ref[...]))
    base = 0
    for o_ref, width in zip(o_refs, widths):
        for c0, cw in _col_chunks(width, col_chunk):
            o_ref[:, c0:c0 + cw] = _dot(xn, w_ref[:, base + c0:base + c0 + cw]).astype(o_ref.dtype)
        base += width


def _norm_matmul(x, nw, w, outs, tm):
    m, d = x.shape
    n = w.shape[1]
    widths = tuple(wd for wd, _ in outs)
    assert sum(widths) == n
    body = functools.partial(_norm_matmul_body, widths=widths, col_chunk=1024)
    res = pl.pallas_call(
        body,
        out_shape=tuple(jax.ShapeDtypeStruct((m, wd), dt) for wd, dt in outs),
        grid=(m // tm,),
        in_specs=[
            pl.BlockSpec((tm, d), lambda i: (i, 0)),
            pl.BlockSpec((1, d), lambda i: (0, 0)),
            pl.BlockSpec((d, n), lambda i: (0, 0), pipeline_mode=pl.Buffered(1)),
        ],
        out_specs=tuple(pl.BlockSpec((tm, wd), lambda i: (i, 0)) for wd, _ in outs),
        compiler_params=pltpu.CompilerParams(
            dimension_semantics=("parallel",),
            vmem_limit_bytes=VMEM_LIMIT),
        name="norm_matmul",
    )(x, nw.reshape(1, d), w)
    return res


def _rope_table_body(pos_ref, freq_ref, cos_ref, sin_ref):
    ang = pos_ref[...].astype(F32) * freq_ref[...]
    cos_ref[...] = jnp.cos(ang)
    sin_ref[...] = jnp.sin(ang)


def _rope_tables(pos, freq_row, tm):
    m = pos.shape[0]
    return pl.pallas_call(
        _rope_table_body,
        out_shape=(jax.ShapeDtypeStruct((m, LANES), F32),
                   jax.ShapeDtypeStruct((m, LANES), F32)),
        grid=(m // tm,),
        in_specs=[
            pl.BlockSpec((tm, 1), lambda i: (i, 0)),
            pl.BlockSpec((1, LANES), lambda i: (0, 0)),
        ],
        out_specs=(pl.BlockSpec((tm, LANES), lambda i: (i, 0)),
                   pl.BlockSpec((tm, LANES), lambda i: (i, 0))),
        compiler_params=pltpu.CompilerParams(dimension_semantics=("parallel",)),
        name="rope_tables",
    )(pos, freq_row)


def _xattn_body(mix_ref, wmix_ref, h_ref, nw_ref, wq_ref, k_ref, v_ref, wo_ref, o_ref, att_ref):
    h = h_ref[...] + _dot(mix_ref[...], wmix_ref[...])
    d = h.shape[-1]
    hd = d // XATTN_HEADS
    hn = _bf(_rms(h, nw_ref[...]))
    q = _bf(_dot(hn, wq_ref[...]) * (hd ** -0.5))
    for i in range(XATTN_HEADS):
        cs = slice(i * hd, (i + 1) * hd)
        s = _dot_nt(q[:, cs], k_ref[0, :, cs])
        s = s - jnp.max(s, axis=-1, keepdims=True)
        p = jnp.exp(s)
        p = p / jnp.sum(p, axis=-1, keepdims=True)
        att_ref[:, cs] = _bf(_dot(_bf(p), v_ref[0, :, cs]))
    o_ref[...] = h + _dot(att_ref[...], wo_ref[...])


def _xattn(mix, w_mix, h, nw, wq, kv, wo, batch, tq):
    m, d = h.shape
    t = m // batch
    mem_len = kv.shape[1]
    nt = t // tq
    row_map = lambda b, i: (b * nt + i, 0)
    const_map = lambda b, i: (0, 0)
    weight = lambda: pl.BlockSpec((d, d), const_map, pipeline_mode=pl.Buffered(1))
    return pl.pallas_call(
        _xattn_body,
        out_shape=jax.ShapeDtypeStruct((m, d), F32),
        grid=(batch, nt),
        in_specs=[
            pl.BlockSpec((tq, d), row_map),
            weight(),
            pl.BlockSpec((tq, d), row_map),
            pl.BlockSpec((1, d), const_map),
            weight(),
            pl.BlockSpec((1, mem_len, d), lambda b, i: (b, 0, 0)),
            pl.BlockSpec((1, mem_len, d), lambda b, i: (b, 0, 1)),
            weight(),
        ],
        out_specs=pl.BlockSpec((tq, d), row_map),
        scratch_shapes=[pltpu.VMEM((tq, d), BF16)],
        compiler_params=pltpu.CompilerParams(
            dimension_semantics=("parallel", "parallel"),
            vmem_limit_bytes=VMEM_LIMIT),
        name="xattn",
    )(mix, w_mix, h, nw.reshape(1, d), wq, kv, kv, wo)


def _shift_rows(x, k, tail, row):
    y = pltpu.roll(x, k, 0)
    for j in range(k):
        src = tail.shape[0] - k + j
        y = jnp.where(row == j, tail[src:src + 1, :], y)
    return y


def _conv_ffn_down_body(u_ref, halo_ref, sh_ref, cw_ref, wd_ref, r_ref, fw_ref, o_ref, act_ref,
                        *, tiles_per_seq, col_chunk, final_norm):
    tm = u_ref.shape[0]
    d_ff = act_ref.shape[1]
    first = (pl.program_id(0) % tiles_per_seq) == 0
    keep = jnp.where(first, 0.0, 1.0)
    row8 = lax.broadcasted_iota(jnp.int32, (8, 1), 0)
    hrows = halo_ref.shape[0]

    def conv(c0):
        cs = slice(c0, c0 + col_chunk)
        xb = u_ref[:, cs]
        w = cw_ref[:, cs]
        sh = _dot(sh_ref[...], xb)
        y = xb.astype(F32) * w[2:3, :] + sh[0:tm] * w[1:2, :] + sh[tm:2 * tm] * w[0:1, :]
        tail = halo_ref[hrows - 8:hrows, cs].astype(F32) * keep
        r1 = pltpu.roll(tail, 1, 0)
        r2 = pltpu.roll(tail, 2, 0)
        fix = jnp.where(row8 == 0, r1 * w[1:2, :] + r2 * w[0:1, :],
                        jnp.where(row8 == 1, r2 * w[0:1, :], 0.0))
        return jnp.concatenate([y[0:8] + fix, y[8:]], axis=0)

    for c0 in range(0, d_ff, col_chunk):
        gate = conv(c0)
        val = conv(d_ff + c0)
        act_ref[:, c0:c0 + col_chunk] = _bf(_silu(gate) * val)
    out = r_ref[...] + _dot(act_ref[...], wd_ref[...])
    if final_norm:
        out = _rms(out, fw_ref[...])
    o_ref[...] = out


def _conv_ffn_down(u, conv_w, w_down, res, final_w, seq_len, tm, final_norm):
    m, n2 = u.shape
    d_ff = n2 // 2
    d = w_down.shape[1]
    halo = 16
    col_chunk = 256
    body = functools.partial(_conv_ffn_down_body, tiles_per_seq=seq_len // tm,
                             col_chunk=col_chunk, final_norm=final_norm)
    t = np.arange(tm)
    shifts = jnp.asarray(np.concatenate([t[None, :] == t[:, None] - k for k in (1, 2)], axis=0),
                         BF16)
    return pl.pallas_call(
        body,
        out_shape=jax.ShapeDtypeStruct((m, d), F32),
        grid=(m // tm,),
        in_specs=[
            pl.BlockSpec((tm, n2), lambda i: (i, 0)),
            pl.BlockSpec((halo, n2), lambda i: (jnp.maximum(i * (tm // halo) - 1, 0), 0)),
            pl.BlockSpec((2 * tm, tm), lambda i: (0, 0)),
            pl.BlockSpec((FFN_CONV, n2), lambda i: (0, 0)),
            pl.BlockSpec((d_ff, d), lambda i: (0, 0), pipeline_mode=pl.Buffered(1)),
            pl.BlockSpec((tm, d), lambda i: (i, 0)),
            pl.BlockSpec((1, d), lambda i: (0, 0)),
        ],
        out_specs=pl.BlockSpec((tm, d), lambda i: (i, 0)),
        scratch_shapes=[pltpu.VMEM((tm, d_ff), BF16)],
        compiler_params=pltpu.CompilerParams(
            dimension_semantics=("parallel",),
            vmem_limit_bytes=VMEM_LIMIT),
        name="conv_ffn_down",
    )(u, u, shifts, conv_w, w_down, res, final_w.reshape(1, d))


def _mixer_constants(tb):
    c = CHUNK
    ar = np.arange
    std_head = ar(GROUP) // HEAD_DIM
    rope_head = (ar(GROUP) % LANES) // ROPE_HALF
    gla_head = ar(GLA_KEYS) // GLA_KEY_DIM
    row_head = ar(GROUP) // c
    s_of = ar(GROUP) % c
    t = ar(c)
    tt = ar(tb) % c
    same_chunk = (ar(tb)[:, None] // c) == (ar(tb)[None, :] // c)

    out = {}
    bd_std = row_head[:, None] == std_head[None, :]
    out["bd_std"] = (bd_std, BF16)
    out["bd_std2"] = (np.concatenate([bd_std, bd_std], axis=1), BF16)
    out["bd_rope"] = (row_head[:, None] == rope_head[None, :], BF16)
    out["bd_gla"] = (row_head[:, None] == gla_head[None, :], BF16)

    log_gamma = np.log(1.0 - np.exp2(-5.0 - ar(N_HEADS, dtype=np.float64)))
    scale = HEAD_DIM ** -0.5
    rel = t[:, None] - s_of[None, :]
    lg_w = log_gamma[std_head][None, :]
    out["ret_d"] = (np.where(rel >= 0, np.exp(lg_w * np.maximum(rel, 0)), 0.0) * scale, F32)
    lg_r = log_gamma[rope_head][None, :]
    out["ret_qdec"] = (np.exp(lg_r * (tt[:, None] + 1.0)), F32)
    out["ret_kdec"] = (np.exp(lg_r * (c - 1.0 - tt[:, None])) * scale, F32)
    out["ret_cdec"] = (np.exp(log_gamma[std_head] * c)[None, :], F32)

    lvl, sel_q, sel_k = [], [], []
    for b in LEVELS:
        blk = t // (2 * b)
        second = (t % (2 * b)) >= b
        bound = blk * 2 * b + b - 1
        msk = (blk[:, None] == blk[None, :]) & second[:, None] & (~second)[None, :]
        lvl.append(msk[:, s_of])
        sel_q.append(second[:, None] & (t[None, :] > bound[:, None]) & (t[None, :] <= t[:, None]))
        sel_k.append((~second)[:, None] & (t[None, :] > t[:, None]) & (t[None, :] <= bound[:, None]))
    lvl.append(t[:, None] == s_of[None, :])
    out["lvl"] = (np.stack(lvl), F32)
    tri = t[None, :] <= t[:, None]
    tail = t[None, :] > t[:, None]
    eye_b = np.eye(tb // c, dtype=bool)
    sel = [np.kron(eye_b, s_) for s_ in sel_q + sel_k + [tri, tail]]
    out["sel"] = (np.stack(sel), BF16)
    out["ones_b"] = (same_chunk, BF16)
    out["trit_w"] = (tt[:, None] <= s_of[None, :], F32)
    out["causal_w"] = (tt[:, None] >= s_of[None, :], F32)
    out["strict_w"] = (tt[:, None] > s_of[None, :], F32)
    out["eye_w"] = (t[:, None] == s_of[None, :], F32)
    eab = np.zeros((LANES, 2 * GROUP))
    eab[SM_BA + std_head, ar(GROUP)] = 1.0
    eab[SM_BB + std_head, GROUP + ar(GROUP)] = 1.0
    out["exp_ab"] = (eab, BF16)
    return out


_CONST_ORDER = ("bd_std", "bd_std2", "bd_rope", "bd_gla", "ret_d", "ret_qdec", "ret_kdec",
                "ret_cdec", "lvl", "sel", "ones_b", "trit_w", "causal_w", "strict_w", "eye_w",
                "exp_ab")
_PARAM_ORDER = ("gdn_conv", "gdn_alog", "gdn_dt", "gdn_nw", "gla_up", "gla_bias", "gla_nw",
                "hg_loglb", "hg_log1mlb", "hg_1mlb", "hg_nw")
_SCRATCH_ORDER = ("sa", "sb", "sct", "sdt", "tail",
                  "aq", "aqd", "ak", "akd", "av",
                  "bq", "bk", "bbr", "brc", "brhs", "bqg", "bkt", "bcd",
                  "cqx", "ckx", "cv", "ccd", "dqx", "dkx", "dv", "dcd", "oscr")


def _mixer_scratch(tb):
    f, b = F32, BF16
    shapes = {
        "sa": ((GROUP, GROUP), f), "sb": ((GROUP, GROUP), f),
        "sct": ((GROUP, GLA_KEYS), f), "sdt": ((GROUP, GROUP), f),
        "tail": ((8, 3 * GROUP), f),
        "aq": ((tb, GROUP), b), "aqd": ((tb, GROUP), b), "ak": ((tb, GROUP), b),
        "akd": ((tb, GROUP), b), "av": ((tb, GROUP), b),
        "bq": ((tb, GROUP), b), "bk": ((tb, GROUP), b), "bbr": ((tb, GROUP), f),
        "brc": ((tb, GROUP), f), "brhs": ((tb, 2 * GROUP), b), "bqg": ((tb, GROUP), b),
        "bkt": ((tb, GROUP), b), "bcd": ((tb, GROUP), f),
        "cqx": ((N_LEVELS + 2, tb, GLA_KEYS), b), "ckx": ((N_LEVELS + 2, tb, GLA_KEYS), b),
        "cv": ((tb, GROUP), b), "ccd": ((tb, GLA_KEYS), f),
        "dqx": ((N_LEVELS + 2, tb, GROUP), b), "dkx": ((N_LEVELS + 2, tb, GROUP), b),
        "dv": ((tb, GROUP), b), "dcd": ((tb, GROUP), f),
        "oscr": ((tb, N_HEADS * GROUP), f),
    }
    return [pltpu.VMEM(*shapes[k]) for k in _SCRATCH_ORDER]


def _mixer_body(*refs, n_chunks):
    zb_ref, zf_ref, cos_ref, sin_ref = refs[:4]
    nc, npar = len(_CONST_ORDER), len(_PARAM_ORDER)
    cst = dict(zip(_CONST_ORDER, refs[4:4 + nc]))
    par = dict(zip(_PARAM_ORDER, refs[4 + nc:4 + nc + npar]))
    o_ref = refs[4 + nc + npar]
    scr = dict(zip(_SCRATCH_ORDER, refs[5 + nc + npar:]))
    tb = zb_ref.shape[0]
    rows = [slice(n * CHUNK, (n + 1) * CHUNK) for n in range(n_chunks)]
    bd_std = cst["bd_std"]

    def zcols(col, width):
        return zb_ref[:, col:col + width].astype(F32)

    @pl.when(pl.program_id(1) == 0)
    def _():
        for k in ("sa", "sb", "sct", "sdt", "tail"):
            scr[k][...] = jnp.zeros_like(scr[k])

    def seg_sum(x):
        return _dot_sel(x, bd_std[...])

    cos = cos_ref[...]
    sin = sin_ref[...]

    def rope(col):
        x1 = zcols(col, LANES)
        x2 = zcols(col + LANES, LANES)
        return jnp.concatenate([x1 * cos - x2 * sin, x1 * sin + x2 * cos], axis=1)

    qr = rope(Z_RQ)
    scr["aq"][...] = _bf(qr)
    scr["aqd"][...] = _bf(qr * cst["ret_qdec"][...])
    kr = rope(Z_RK)
    scr["ak"][...] = _bf(kr)
    scr["akd"][...] = _bf(kr * cst["ret_kdec"][...])
    scr["av"][...] = zb_ref[:, Z_RV:Z_RV + GROUP]

    row = lax.broadcasted_iota(jnp.int32, (tb, 1), 0)
    x = zcols(Z_BQKV, 3 * GROUP)
    tail = scr["tail"][...]
    cw = par["gdn_conv"][...]
    y = x * cw[GDN_CONV - 1:GDN_CONV, :]
    for k in range(1, GDN_CONV):
        y = y + _shift_rows(x, k, tail, row) * cw[GDN_CONV - 1 - k:GDN_CONV - k, :]
    scr["tail"][...] = x[tb - 8:tb, :]
    qkv = _silu(y)
    qb, kb, vv = qkv[:, 0:GROUP], qkv[:, GROUP:2 * GROUP], qkv[:, 2 * GROUP:3 * GROUP]
    sm = zf_ref[:, F_SMALL:F_SMALL + LANES]
    bab = _dot_sel(sm, cst["exp_ab"][...])
    g = -jnp.exp(par["gdn_alog"][...]) * _softplus(bab[:, 0:GROUP] + par["gdn_dt"][...])
    beta = jax.nn.sigmoid(bab[:, GROUP:2 * GROUP])
    qn = qb * lax.rsqrt(seg_sum(qb * qb) + EPS) * (HEAD_DIM ** -0.5)
    kn = kb * lax.rsqrt(seg_sum(kb * kb) + EPS)
    ones_b = cst["ones_b"][...]
    gc = _sel_dot(cst["sel"][2 * N_LEVELS], g)
    gcr = _sel_dot(ones_b, g * cst["trit_w"][...])
    glast = _sel_dot(ones_b, g)
    dec = jnp.exp(gc - gcr)
    gam = jnp.exp(gc)
    scr["bq"][...] = _bf(qn)
    scr["bk"][...] = _bf(kn)
    scr["bbr"][...] = beta * jnp.where(cst["strict_w"][...] > 0, dec, 0.0)
    scr["brc"][...] = jnp.where(cst["causal_w"][...] > 0, dec, 0.0)
    scr["brhs"][...] = _bf(jnp.concatenate([beta * vv, beta * gam * kn], axis=1))
    scr["bqg"][...] = _bf(qn * gam)
    scr["bkt"][...] = _bf(kn * jnp.exp(glast - gc))
    scr["bcd"][...] = jnp.exp(glast)

    def gl_prep(q, k, v, logf, qx_ref, kx_ref, v_ref, cd_ref):
        hi, lo = _split(logf, 2)
        for j in range(2 * N_LEVELS + 2):
            sel = cst["sel"][j]
            e = jnp.exp(_dot(sel, hi) + _dot(sel, lo))
            if j < N_LEVELS:
                qx_ref[j] = _bf(q * e)
            elif j < 2 * N_LEVELS:
                kx_ref[j - N_LEVELS] = _bf(k * e)
            elif j == 2 * N_LEVELS:
                qx_ref[GL_EDGE] = _bf(q * e)
                cd_ref[...] = e
            else:
                kx_ref[GL_EDGE] = _bf(k * e)
        qx_ref[GL_DIAG] = _bf(q)
        kx_ref[GL_DIAG] = _bf(k)
        v_ref[...] = _bf(v)

    logit = _dot_hp(sm, par["gla_up"][...]) + par["gla_bias"][...]
    gl_prep(zcols(Z_CQ, GLA_KEYS) * (GLA_KEY_DIM ** -0.5),
            zcols(Z_CK, GLA_KEYS), zb_ref[:, Z_CV:Z_CV + GROUP],
            _log_sigmoid(logit) * (1.0 / GLA_GATE_NORM),
            scr["cqx"], scr["ckx"], scr["cv"], scr["ccd"])

    f_pre = zf_ref[:, F_DF:F_DF + GROUP]
    a = par["hg_loglb"][...]
    b = par["hg_log1mlb"][...] + _log_sigmoid(f_pre)
    logf_d = jnp.maximum(a, b) + jnp.log1p(jnp.exp(-jnp.abs(a - b)))
    gl_prep(zcols(Z_DQ, GROUP), par["hg_1mlb"][...] * jax.nn.sigmoid(-f_pre),
            zb_ref[:, Z_DI:Z_DI + GROUP], logf_d,
            scr["dqx"], scr["dkx"], scr["dv"], scr["dcd"])

    pa, pb, pc, pd = {}, {}, {}, {}

    def ret_p1(n):
        rs = rows[n]
        p = _dot_nt(scr["aq"][rs, :], _blockdiag(scr["ak"][rs, :], cst["bd_rope"][...]))
        yield
        pa[n] = _bf(p * cst["ret_d"][...])

    def gl_p1(n, qx_ref, kx_ref, bd_k, store):
        rs = rows[n]
        p = None
        for l in range(N_LEVELS + 1):
            t = _dot_nt(qx_ref[l, rs, :], _blockdiag(kx_ref[l, rs, :], bd_k[...]))
            yield
            t = t * cst["lvl"][l]
            p = t if p is None else p + t
        store[n] = _bf(p)

    def wide_matmul(a, b):
        return _dot(_bf(a), _blockdiag(_bf(b), bd_std[...]))

    def gdn_p1(n):
        rs = rows[n]
        knb = scr["bk"][rs, :]
        kbd = _blockdiag(knb, bd_std[...])
        kk = _dot_nt(knb, kbd)
        qk = _dot_nt(scr["bq"][rs, :], kbd)
        yield
        lmat = scr["bbr"][rs, :] * kk
        tinv = cst["eye_w"][...] - lmat * cst["lvl"][0]
        for li in range(1, N_LEVELS):
            xb = wide_matmul(lmat * cst["lvl"][li], tinv)
            yield
            tinv = tinv - wide_matmul(tinv, xb)
            yield
        uw = _dot(_bf(tinv), _blockdiag(scr["brhs"][rs, :], cst["bd_std2"][...]))
        yield
        pb[n] = (uw[:, 0:GROUP], _bf(uw[:, GROUP:2 * GROUP]), _bf(qk * scr["brc"][rs, :]))

    gens = []
    for n in range(n_chunks):
        gens += [gdn_p1(n), gl_p1(n, scr["cqx"], scr["ckx"], cst["bd_gla"], pc),
                 gl_p1(n, scr["dqx"], scr["dkx"], bd_std, pd), ret_p1(n)]
    _run_interleaved(gens)

    oscr = scr["oscr"]

    def ret_p2():
        s = scr["sa"][...]
        for n, rs in enumerate(rows):
            vbd = _blockdiag(scr["av"][rs, :], bd_std[...])
            o = _dot(pa[n], vbd) + _dot(scr["aqd"][rs, :], _bf(s))
            kv = _dot_tn(_blockdiag(scr["akd"][rs, :], cst["bd_rope"][...]), vbd)
            yield
            s = s * cst["ret_cdec"][...] + kv
            oscr[rs, 0:GROUP] = o
        scr["sa"][...] = s

    def gdn_p2():
        s = scr["sb"][...]
        for n, rs in enumerate(rows):
            u_v, w_k, qk = pb[n]
            r = _dot(jnp.concatenate([w_k, scr["bqg"][rs, :]], axis=0), _bf(s))
            yield
            u = u_v - r[0:CHUNK]
            ubd = _blockdiag(_bf(u), bd_std[...])
            o = r[CHUNK:2 * CHUNK] + _dot(qk, ubd)
            kv = _dot_tn(_blockdiag(scr["bkt"][rs, :], bd_std[...]), ubd)
            yield
            s = s * scr["bcd"][rs.start:rs.start + 1, :] + kv
            oscr[rs, GROUP:2 * GROUP] = o
        scr["sb"][...] = s

    def gl_p2(store, qx_ref, kx_ref, v_ref, cd_ref, bd_k, st_ref, col):
        st = st_ref[...]
        for n, rs in enumerate(rows):
            vbd = _blockdiag(v_ref[rs, :], bd_std[...])
            o = _dot(store[n], vbd) + _dot_nt(qx_ref[GL_EDGE, rs, :], _bf(st))
            kv = _dot_tn(vbd, _blockdiag(kx_ref[GL_EDGE, rs, :], bd_k[...]))
            yield
            st = st * cd_ref[rs.stop - 1:rs.stop, :] + kv
            oscr[rs, col:col + GROUP] = o
        st_ref[...] = st

    _run_interleaved([
        gdn_p2(), ret_p2(),
        gl_p2(pc, scr["cqx"], scr["ckx"], scr["cv"], scr["ccd"], cst["bd_gla"], scr["sct"],
              2 * GROUP),
        gl_p2(pd, scr["dqx"], scr["dkx"], scr["dv"], scr["dcd"], bd_std, scr["sdt"],
              3 * GROUP)])

    inv_hd = 1.0 / HEAD_DIM
    gains = (None, par["gdn_nw"], par["gla_nw"], par["hg_nw"])
    gates = (Z_RG, Z_BG, Z_CG, Z_DG)
    for i in range(4):
        cs = slice(i * GROUP, (i + 1) * GROUP)
        o = oscr[:, cs]
        o = o * lax.rsqrt(seg_sum(o * o) * inv_hd + EPS)
        if gains[i] is not None:
            o = o * gains[i][...]
        o_ref[:, cs] = _bf(o * _silu(zcols(gates[i], GROUP)))


def _mixer(zb, zf, cos_t, sin_t, params, batch, tb):
    m = zb.shape[0]
    t = m // batch
    nt = t // tb
    consts = {k: jnp.asarray(v, dt) for k, (v, dt) in _mixer_constants(tb).items()}
    const_args = [consts[k] for k in _CONST_ORDER]
    param_args = [params[k] for k in _PARAM_ORDER]

    def full_spec(a):
        nd = a.ndim
        return pl.BlockSpec(a.shape, lambda b, i, _nd=nd: (0,) * _nd)

    row_map = lambda b, i: (b * nt + i, 0)
    body = functools.partial(_mixer_body, n_chunks=tb // CHUNK)
    return pl.pallas_call(
        body,
        out_shape=jax.ShapeDtypeStruct((m, N_HEADS * GROUP), BF16),
        grid=(batch, nt),
        in_specs=[pl.BlockSpec((tb, Z_WIDTH), row_map),
                  pl.BlockSpec((tb, F_WIDTH), row_map),
                  pl.BlockSpec((tb, LANES), row_map),
                  pl.BlockSpec((tb, LANES), row_map)]
                 + [full_spec(a) for a in const_args]
                 + [full_spec(a) for a in param_args],
        out_specs=pl.BlockSpec((tb, N_HEADS * GROUP), row_map),
        scratch_shapes=_mixer_scratch(tb),
        compiler_params=pltpu.CompilerParams(
            dimension_semantics=("parallel", "arbitrary"),
            vmem_limit_bytes=VMEM_LIMIT),
        name="mixer",
    )(zb, zf, cos_t, sin_t, *const_args, *param_args)


def _in_proj_segments():
    src = {}
    off = 0
    for name, width in (("rq", 256), ("rk", 256), ("rv", 256), ("rg", 256),
                        ("bq", 256), ("bk", 256), ("bv", 256), ("ba", 4), ("bb", 4), ("bg", 256),
                        ("cq", 128), ("ck", 128), ("cv", 256), ("clr", 16), ("cg", 256),
                        ("dq", 256), ("df", 256), ("di", 256), ("dg", 256)):
        src[name] = off
        off += width

    def rope(name):
        return [(src[name] + h * HEAD_DIM + half * ROPE_HALF, ROPE_HALF)
                for half in range(2) for h in range(N_HEADS)]

    segs = rope("rq") + rope("rk")
    segs += [(src["rv"], 256), (src["rg"], 256), (src["bq"], 768), (src["bg"], 256),
             (src["cq"], 128), (src["ck"], 128), (src["cv"], 256), (src["cg"], 256),
             (src["dq"], 256), (src["di"], 256), (src["dg"], 256)]
    assert sum(w for _, w in segs) == Z_WIDTH
    segs += [(src["df"], 256), (src["ba"], 8), (src["clr"], GLA_LOWRANK),
             (None, LANES - SM_LR - GLA_LOWRANK)]
    assert sum(w for _, w in segs) == Z_WIDTH + F_WIDTH
    return segs


def kernel(x, mem, positions, mix_norm_w, w_in, gdn_conv_w, gdn_a_log, gdn_dt_bias, gdn_norm_w,
           gla_gk_up, gla_gk_bias, gla_norm_w, hgrn_lb_logits, hgrn_norm_w, w_out,
           xattn_norm_w, mem_norm_w, xattn_wq, xattn_wk, xattn_wv, xattn_wo,
           ffn_norm_w, ffn_up, ffn_conv_w, ffn_down, final_norm_w):
    batch, seq, d = x.shape
    depth = w_in.shape[0]
    m = batch * seq
    mem_len = mem.shape[1]

    w_in_p = jnp.concatenate(
        [jnp.zeros((depth, d, wd), BF16) if s0 is None else w_in[:, :, s0:s0 + wd].astype(BF16)
         for s0, wd in _in_proj_segments()], axis=2)

    freqs = ROPE_BASE ** (-jnp.arange(0, HEAD_DIM, 2, dtype=F32) / HEAD_DIM)
    freq_row = jnp.tile(freqs, N_HEADS).reshape(1, LANES)
    cos_t, sin_t = _rope_tables(positions.reshape(m, 1), freq_row, 1024)

    lb_all = jnp.cumsum(jax.nn.softmax(hgrn_lb_logits.astype(F32), axis=0), axis=0)
    lb_all = lb_all - lb_all[0]

    def head_row(v):
        return jnp.tile(v.astype(F32), N_HEADS).reshape(1, GROUP)

    h = x.reshape(m, d)
    mem2 = mem.reshape(batch * mem_len, d)
    for l in range(depth):
        gla_up = jnp.zeros((LANES, GLA_KEYS), F32).at[SM_LR:SM_LR + GLA_LOWRANK].set(gla_gk_up[l])
        params = {
            "gdn_conv": gdn_conv_w[l].astype(F32),
            "gdn_alog": jnp.repeat(gdn_a_log[l].astype(F32), HEAD_DIM).reshape(1, GROUP),
            "gdn_dt": jnp.repeat(gdn_dt_bias[l].astype(F32), HEAD_DIM).reshape(1, GROUP),
            "gdn_nw": head_row(gdn_norm_w[l]),
            "gla_up": gla_up,
            "gla_bias": gla_gk_bias[l].astype(F32).reshape(1, GLA_KEYS),
            "gla_nw": head_row(gla_norm_w[l]),
            "hg_loglb": jnp.log(lb_all[l]).reshape(1, GROUP),
            "hg_log1mlb": jnp.log1p(-lb_all[l]).reshape(1, GROUP),
            "hg_1mlb": (1.0 - lb_all[l]).reshape(1, GROUP),
            "hg_nw": head_row(hgrn_norm_w[l]),
        }
        zb, zf = _norm_matmul(h, mix_norm_w[l], w_in_p[l],
                              ((Z_WIDTH, BF16), (F_WIDTH, F32)), 512)
        o = _mixer(zb, zf, cos_t, sin_t, params, batch, 256)

        w_kv = jnp.concatenate([xattn_wk[l], xattn_wv[l]], axis=1).astype(BF16)
        kv, = _norm_matmul(mem2, mem_norm_w, w_kv, ((2 * d, BF16),), 512)
        h = _xattn(o, w_out[l].astype(BF16), h, xattn_norm_w[l], xattn_wq[l].astype(BF16),
                   kv.reshape(batch, mem_len, 2 * d), xattn_wo[l].astype(BF16), batch, 512)

        u, = _norm_matmul(h, ffn_norm_w[l], ffn_up[l].astype(BF16), ((ffn_up.shape[2], BF16),), 512)
        h = _conv_ffn_down(u, ffn_conv_w[l].astype(F32), ffn_down[l].astype(BF16), h,
                           final_norm_w, seq, 256, l == depth - 1)
    return h.reshape(batch, seq, d)
```

```python
import functools

import numpy as np
import jax
import jax.numpy as jnp
from jax import lax
from jax.experimental import pallas as pl
from jax.experimental.pallas import tpu as pltpu

F32 = jnp.float32
BF16 = jnp.bfloat16

N_HEADS = 4
HEAD_DIM = 64
GROUP = N_HEADS * HEAD_DIM
GLA_KEY_DIM = 32
GLA_KEYS = N_HEADS * GLA_KEY_DIM
GLA_LOWRANK = 16
GLA_GATE_NORM = 16.0
GDN_CONV = 4
CHUNK = 64
ROPE_BASE = 10000.0
ROPE_HALF = HEAD_DIM // 2
XATTN_HEADS = 4
FFN_CONV = 3
CONV_PHASES = 4
EPS = 1e-6

LANES = 128
VMEM_LIMIT = 52 * 1024 * 1024

Z_RQ, Z_RK, Z_RV, Z_RG = 0, 256, 512, 768
Z_BQKV, Z_BG = 1024, 1792
Z_CQ, Z_CK, Z_CV, Z_CG = 2048, 2176, 2304, 2560
Z_DQ, Z_DI, Z_DG = 2816, 3072, 3328
Z_WIDTH = 3584
F_DF, F_SMALL = 0, 256
F_WIDTH = 384
SM_BA, SM_BB, SM_LR = 0, 4, 8

LEVELS = (1, 2, 4, 8, 16, 32)
N_LEVELS = len(LEVELS)
GL_DIAG = N_LEVELS
GL_EDGE = N_LEVELS + 1


def _bf(x):
    return x.astype(BF16)


def _dot(a, b):
    return jnp.dot(a, b, preferred_element_type=F32)


def _dot_nt(a, b):
    return lax.dot_general(a, b, (((1,), (1,)), ((), ())), preferred_element_type=F32)


def _dot_tn(a, b):
    return lax.dot_general(a, b, (((0,), (0,)), ((), ())), preferred_element_type=F32)


def _split(x, n):
    parts = []
    r = x
    for i in range(n):
        p = r.astype(BF16)
        parts.append(p)
        if i < n - 1:
            r = r - p.astype(F32)
    return parts


def _sel_dot(c, x, n=2):
    acc = None
    for p in _split(x, n):
        t = _dot(c, p)
        acc = t if acc is None else acc + t
    return acc


def _dot_sel(x, c, n=2):
    acc = None
    for p in _split(x, n):
        t = _dot(p, c)
        acc = t if acc is None else acc + t
    return acc


def _dot_hp(a, b):
    a_hi, a_lo = _split(a, 2)
    b_hi, b_lo = _split(b, 2)
    return _dot(a_hi, b_hi) + _dot(a_lo, b_hi) + _dot(a_hi, b_lo)


def _silu(x):
    return x * jax.nn.sigmoid(x)


def _log_sigmoid(x):
    return jnp.minimum(x, 0.0) - jnp.log1p(jnp.exp(-jnp.abs(x)))


def _softplus(x):
    return jnp.maximum(x, 0.0) + jnp.log1p(jnp.exp(-jnp.abs(x)))


def _rms(x, w):
    ms = jnp.mean(x * x, axis=-1, keepdims=True)
    return x * lax.rsqrt(ms + EPS) * w


def _blockdiag(x, mask):
    return jnp.concatenate([x, x, x, x], axis=0) * mask


def _run_interleaved(gens):
    gens = list(gens)
    while gens:
        alive = []
        for g in gens:
            try:
                next(g)
                alive.append(g)
            except StopIteration:
                pass
        gens = alive


def _col_chunks(n, width):
    return [(c0, min(width, n - c0)) for c0 in range(0, n, width)]


def _norm_matmul_body(x_ref, nw_ref, w_ref, *o_refs, widths, col_chunk):
    xn = _bf(_rms(x_ref[...], nw_ref[...]))
    base = 0
    for o_ref, width in zip(o_refs, widths):
        for c0, cw in _col_chunks(width, col_chunk):
            o_ref[:, c0:c0 + cw] = _dot(xn, w_ref[:, base + c0:base + c0 + cw]).astype(o_ref.dtype)
        base += width


def _norm_matmul(x, nw, w, outs, tm):
    m, d = x.shape
    n = w.shape[1]
    widths = tuple(wd for wd, _ in outs)
    assert sum(widths) == n
    body = functools.partial(_norm_matmul_body, widths=widths, col_chunk=1024)
    res = pl.pallas_call(
        body,
        out_shape=tuple(jax.ShapeDtypeStruct((m, wd), dt) for wd, dt in outs),
        grid=(m // tm,),
        in_specs=[
            pl.BlockSpec((tm, d), lambda i: (i, 0)),
            pl.BlockSpec((1, d), lambda i: (0, 0)),
            pl.BlockSpec((d, n), lambda i: (0, 0), pipeline_mode=pl.Buffered(1)),
        ],
        out_specs=tuple(pl.BlockSpec((tm, wd), lambda i: (i, 0)) for wd, _ in outs),
        compiler_params=pltpu.CompilerParams(
            dimension_semantics=("parallel",),
            vmem_limit_bytes=VMEM_LIMIT),
        name="norm_matmul",
    )(x, nw.reshape(1, d), w)
    return res


def _rope_table_body(pos_ref, freq_ref, cos_ref, sin_ref):
    ang = pos_ref[...].astype(F32) * freq_ref[...]
    cos_ref[...] = jnp.cos(ang)
    sin_ref[...] = jnp.sin(ang)


def _rope_tables(pos, freq_row, tm):
    m = pos.shape[0]
    return pl.pallas_call(
        _rope_table_body,
        out_shape=(jax.ShapeDtypeStruct((m, LANES), F32),
                   jax.ShapeDtypeStruct((m, LANES), F32)),
        grid=(m // tm,),
        in_specs=[
            pl.BlockSpec((tm, 1), lambda i: (i, 0)),
            pl.BlockSpec((1, LANES), lambda i: (0, 0)),
        ],
        out_specs=(pl.BlockSpec((tm, LANES), lambda i: (i, 0)),
                   pl.BlockSpec((tm, LANES), lambda i: (i, 0))),
        compiler_params=pltpu.CompilerParams(dimension_semantics=("parallel",)),
        name="rope_tables",
    )(pos, freq_row)


def _xattn_body(mix_ref, wmix_ref, h_ref, nw_ref, wq_ref, k_ref, v_ref, wo_ref, o_ref, att_ref):
    h = h_ref[...] + _dot(mix_ref[...], wmix_ref[...])
    d = h.shape[-1]
    hd = d // XATTN_HEADS
    hn = _bf(_rms(h, nw_ref[...]))
    q = _bf(_dot(hn, wq_ref[...]) * (hd ** -0.5))
    for i in range(XATTN_HEADS):
        cs = slice(i * hd, (i + 1) * hd)
        s = _dot_nt(q[:, cs], k_ref[0, :, cs])
        s = s - jnp.max(s, axis=-1, keepdims=True)
        p = jnp.exp(s)
        p = p / jnp.sum(p, axis=-1, keepdims=True)
        att_ref[:, cs] = _bf(_dot(_bf(p), v_ref[0, :, cs]))
    o_ref[...] = h + _dot(att_ref[...], wo_ref[...])


def _xattn(mix, w_mix, h, nw, wq, kv, wo, batch, tq):
    m, d = h.shape
    t = m // batch
    mem_len = kv.shape[1]
    nt = t // tq
    row_map = lambda b, i: (b * nt + i, 0)
    const_map = lambda b, i: (0, 0)
    weight = lambda: pl.BlockSpec((d, d), const_map, pipeline_mode=pl.Buffered(1))
    return pl.pallas_call(
        _xattn_body,
        out_shape=jax.ShapeDtypeStruct((m, d), F32),
        grid=(batch, nt),
        in_specs=[
            pl.BlockSpec((tq, d), row_map),
            weight(),
            pl.BlockSpec((tq, d), row_map),
            pl.BlockSpec((1, d), const_map),
            weight(),
            pl.BlockSpec((1, mem_len, d), lambda b, i: (b, 0, 0)),
            pl.BlockSpec((1, mem_len, d), lambda b, i: (b, 0, 1)),
            weight(),
        ],
        out_specs=pl.BlockSpec((tq, d), row_map),
        scratch_shapes=[pltpu.VMEM((tq, d), BF16)],
        compiler_params=pltpu.CompilerParams(
            dimension_semantics=("parallel", "parallel"),
            vmem_limit_bytes=VMEM_LIMIT),
        name="xattn",
    )(mix, w_mix, h, nw.reshape(1, d), wq, kv, kv, wo)


def _shift_rows(x, k, tail, row):
    y = pltpu.roll(x, k, 0)
    for j in range(k):
        src = tail.shape[0] - k + j
        y = jnp.where(row == j, tail[src:src + 1, :], y)
    return y


def _conv_ffn_down_body(u_ref, halo_ref, cw_ref, wd_ref, r_ref, fw_ref, o_ref,
                        ext_ref, yst_ref, act_ref, *, tiles_per_seq, final_norm):
    tm = u_ref.shape[0]
    d_ff = act_ref.shape[1]
    first = (pl.program_id(0) % tiles_per_seq) == 0
    keep = jnp.where(first, 0.0, 1.0)
    hrows = halo_ref.shape[0]
    nq = tm // CONV_PHASES

    def phase_rows(half, off):
        return ext_ref[half, pl.ds(8 + off, nq, stride=CONV_PHASES), :]

    for c0 in range(0, d_ff, LANES):
        for half in range(2):
            cs = slice(half * d_ff + c0, half * d_ff + c0 + LANES)
            ext_ref[half, 0:8, :] = halo_ref[hrows - 8:hrows, cs].astype(F32) * keep
            ext_ref[half, 8:8 + tm, :] = u_ref[:, cs].astype(F32)
        xs = [[phase_rows(half, off) for off in range(1 - FFN_CONV, CONV_PHASES)]
              for half in range(2)]
        ws = [cw_ref[:, half * d_ff + c0:half * d_ff + c0 + LANES] for half in range(2)]
        for p in range(CONV_PHASES):
            ys = []
            for half in range(2):
                y = None
                for k in range(FFN_CONV):
                    t = xs[half][p + k] * ws[half][k:k + 1, :]
                    y = t if y is None else y + t
                ys.append(y)
            hg = ys[0]
            yst_ref[pl.ds(p, nq, stride=CONV_PHASES), :] = (hg + hg * jnp.tanh(hg)) * ys[1]
        act_ref[:, c0:c0 + LANES] = _bf(yst_ref[...])
    out = r_ref[...] + _dot(act_ref[...], wd_ref[...])
    if final_norm:
        out = _rms(out, fw_ref[...])
    o_ref[...] = out


def _conv_ffn_down(u, conv_w, w_down, res, final_w, seq_len, tm, final_norm):
    m, n2 = u.shape
    d_ff = n2 // 2
    d = w_down.shape[1]
    halo = 16
    body = functools.partial(_conv_ffn_down_body, tiles_per_seq=seq_len // tm,
                             final_norm=final_norm)
    return pl.pallas_call(
        body,
        out_shape=jax.ShapeDtypeStruct((m, d), F32),
        grid=(m // tm,),
        in_specs=[
            pl.BlockSpec((tm, n2), lambda i: (i, 0)),
            pl.BlockSpec((halo, n2), lambda i: (jnp.maximum(i * (tm // halo) - 1, 0), 0)),
            pl.BlockSpec((FFN_CONV, n2), lambda i: (0, 0)),
            pl.BlockSpec((d_ff, d), lambda i: (0, 0), pipeline_mode=pl.Buffered(1)),
            pl.BlockSpec((tm, d), lambda i: (i, 0)),
            pl.BlockSpec((1, d), lambda i: (0, 0)),
        ],
        out_specs=pl.BlockSpec((tm, d), lambda i: (i, 0)),
        scratch_shapes=[pltpu.VMEM((2, 8 + tm, LANES), F32),
                        pltpu.VMEM((tm, LANES), F32),
                        pltpu.VMEM((tm, d_ff), BF16)],
        compiler_params=pltpu.CompilerParams(
            dimension_semantics=("parallel",),
            vmem_limit_bytes=VMEM_LIMIT),
        name="conv_ffn_down",
    )(u, u, conv_w, w_down, res, final_w.reshape(1, d))


def _mixer_constants(tb):
    c = CHUNK
    ar = np.arange
    std_head = ar(GROUP) // HEAD_DIM
    rope_head = (ar(GROUP) % LANES) // ROPE_HALF
    gla_head = ar(GLA_KEYS) // GLA_KEY_DIM
    row_head = ar(GROUP) // c
    s_of = ar(GROUP) % c
    t = ar(c)
    tt = ar(tb) % c
    same_chunk = (ar(tb)[:, None] // c) == (ar(tb)[None, :] // c)

    out = {}
    bd_std = row_head[:, None] == std_head[None, :]
    out["bd_std"] = (bd_std, BF16)
    out["bd_std2"] = (np.concatenate([bd_std, bd_std], axis=1), BF16)
    out["bd_rope"] = (row_head[:, None] == rope_head[None, :], BF16)
    out["bd_gla"] = (row_head[:, None] == gla_head[None, :], BF16)

    log_gamma = np.log(1.0 - np.exp2(-5.0 - ar(N_HEADS, dtype=np.float64)))
    scale = HEAD_DIM ** -0.5
    rel = t[:, None] - s_of[None, :]
    lg_w = log_gamma[std_head][None, :]
    out["ret_d"] = (np.where(rel >= 0, np.exp(lg_w * np.maximum(rel, 0)), 0.0) * scale, F32)
    lg_r = log_gamma[rope_head][None, :]
    out["ret_qdec"] = (np.exp(lg_r * (tt[:, None] + 1.0)), F32)
    out["ret_kdec"] = (np.exp(lg_r * (c - 1.0 - tt[:, None])) * scale, F32)
    out["ret_cdec"] = (np.exp(log_gamma[std_head] * c)[None, :], F32)

    lvl, sel_q, sel_k = [], [], []
    for b in LEVELS:
        blk = t // (2 * b)
        second = (t % (2 * b)) >= b
        bound = blk * 2 * b + b - 1
        msk = (blk[:, None] == blk[None, :]) & second[:, None] & (~second)[None, :]
        lvl.append(msk[:, s_of])
        sel_q.append(second[:, None] & (t[None, :] > bound[:, None]) & (t[None, :] <= t[:, None]))
        sel_k.append((~second)[:, None] & (t[None, :] > t[:, None]) & (t[None, :] <= bound[:, None]))
    lvl.append(t[:, None] == s_of[None, :])
    out["lvl"] = (np.stack(lvl), F32)
    tri = t[None, :] <= t[:, None]
    tail = t[None, :] > t[:, None]
    eye_b = np.eye(tb // c, dtype=bool)
    sel = [np.kron(eye_b, s_) for s_ in sel_q + sel_k + [tri, tail]]
    out["sel"] = (np.stack(sel), BF16)
    out["ones_b"] = (same_chunk, BF16)
    out["trit_w"] = (tt[:, None] <= s_of[None, :], F32)
    out["causal_w"] = (tt[:, None] >= s_of[None, :], F32)
    out["strict_w"] = (tt[:, None] > s_of[None, :], F32)
    out["eye_w"] = (t[:, None] == s_of[None, :], F32)
    eab = np.zeros((LANES, 2 * GROUP))
    eab[SM_BA + std_head, ar(GROUP)] = 1.0
    eab[SM_BB + std_head, GROUP + ar(GROUP)] = 1.0
    out["exp_ab"] = (eab, BF16)
    return out


_CONST_ORDER = ("bd_std", "bd_std2", "bd_rope", "bd_gla", "ret_d", "ret_qdec", "ret_kdec",
                "ret_cdec", "lvl", "sel", "ones_b", "trit_w", "causal_w", "strict_w", "eye_w",
                "exp_ab")
_PARAM_ORDER = ("gdn_conv", "gdn_alog", "gdn_dt", "gdn_nw", "gla_up", "gla_bias", "gla_nw",
                "hg_loglb", "hg_log1mlb", "hg_1mlb", "hg_nw")
_SCRATCH_ORDER = ("sa", "sb", "sct", "sdt", "tail",
                  "aq", "aqd", "ak", "akd", "av",
                  "bq", "bk", "bbr", "brc", "brhs", "bqg", "bkt", "bcd",
                  "cqx", "ckx", "cv", "ccd", "dqx", "dkx", "dv", "dcd", "oscr")


def _mixer_scratch(tb):
    f, b = F32, BF16
    shapes = {
        "sa": ((GROUP, GROUP), f), "sb": ((GROUP, GROUP), f),
        "sct": ((GROUP, GLA_KEYS), f), "sdt": ((GROUP, GROUP), f),
        "tail": ((8, 3 * GROUP), f),
        "aq": ((tb, GROUP), b), "aqd": ((tb, GROUP), b), "ak": ((tb, GROUP), b),
        "akd": ((tb, GROUP), b), "av": ((tb, GROUP), b),
        "bq": ((tb, GROUP), b), "bk": ((tb, GROUP), b), "bbr": ((tb, GROUP), f),
        "brc": ((tb, GROUP), f), "brhs": ((tb, 2 * GROUP), b), "bqg": ((tb, GROUP), b),
        "bkt": ((tb, GROUP), b), "bcd": ((tb, GROUP), f),
        "cqx": ((N_LEVELS + 2, tb, GLA_KEYS), b), "ckx": ((N_LEVELS + 2, tb, GLA_KEYS), b),
        "cv": ((tb, GROUP), b), "ccd": ((tb, GLA_KEYS), f),
        "dqx": ((N_LEVELS + 2, tb, GROUP), b), "dkx": ((N_LEVELS + 2, tb, GROUP), b),
        "dv": ((tb, GROUP), b), "dcd": ((tb, GROUP), f),
        "oscr": ((tb, N_HEADS * GROUP), f),
    }
    return [pltpu.VMEM(*shapes[k]) for k in _SCRATCH_ORDER]


def _mixer_body(*refs, n_chunks):
    zb_ref, zf_ref, cos_ref, sin_ref = refs[:4]
    nc, npar = len(_CONST_ORDER), len(_PARAM_ORDER)
    cst = dict(zip(_CONST_ORDER, refs[4:4 + nc]))
    par = dict(zip(_PARAM_ORDER, refs[4 + nc:4 + nc + npar]))
    o_ref = refs[4 + nc + npar]
    scr = dict(zip(_SCRATCH_ORDER, refs[5 + nc + npar:]))
    tb = zb_ref.shape[0]
    rows = [slice(n * CHUNK, (n + 1) * CHUNK) for n in range(n_chunks)]
    bd_std = cst["bd_std"]

    def zcols(col, width):
        return zb_ref[:, col:col + width].astype(F32)

    @pl.when(pl.program_id(1) == 0)
    def _():
        for k in ("sa", "sb", "sct", "sdt", "tail"):
            scr[k][...] = jnp.zeros_like(scr[k])

    def seg_sum(x):
        return _dot_sel(x, bd_std[...])

    cos = cos_ref[...]
    sin = sin_ref[...]

    def rope(col):
        x1 = zcols(col, LANES)
        x2 = zcols(col + LANES, LANES)
        return jnp.concatenate([x1 * cos - x2 * sin, x1 * sin + x2 * cos], axis=1)

    qr = rope(Z_RQ)
    scr["aq"][...] = _bf(qr)
    scr["aqd"][...] = _bf(qr * cst["ret_qdec"][...])
    kr = rope(Z_RK)
    scr["ak"][...] = _bf(kr)
    scr["akd"][...] = _bf(kr * cst["ret_kdec"][...])
    scr["av"][...] = zb_ref[:, Z_RV:Z_RV + GROUP]

    row = lax.broadcasted_iota(jnp.int32, (tb, 1), 0)
    x = zcols(Z_BQKV, 3 * GROUP)
    tail = scr["tail"][...]
    cw = par["gdn_conv"][...]
    y = x * cw[GDN_CONV - 1:GDN_CONV, :]
    for k in range(1, GDN_CONV):
        y = y + _shift_rows(x, k, tail, row) * cw[GDN_CONV - 1 - k:GDN_CONV - k, :]
    scr["tail"][...] = x[tb - 8:tb, :]
    qkv = _silu(y)
    qb, kb, vv = qkv[:, 0:GROUP], qkv[:, GROUP:2 * GROUP], qkv[:, 2 * GROUP:3 * GROUP]
    sm = zf_ref[:, F_SMALL:F_SMALL + LANES]
    bab = _dot_sel(sm, cst["exp_ab"][...])
    g = -jnp.exp(par["gdn_alog"][...]) * _softplus(bab[:, 0:GROUP] + par["gdn_dt"][...])
    beta = jax.nn.sigmoid(bab[:, GROUP:2 * GROUP])
    qn = qb * lax.rsqrt(seg_sum(qb * qb) + EPS) * (HEAD_DIM ** -0.5)
    kn = kb * lax.rsqrt(seg_sum(kb * kb) + EPS)
    ones_b = cst["ones_b"][...]
    gc = _sel_dot(cst["sel"][2 * N_LEVELS], g)
    gcr = _sel_dot(ones_b, g * cst["trit_w"][...])
    glast = _sel_dot(ones_b, g)
    dec = jnp.exp(gc - gcr)
    gam = jnp.exp(gc)
    scr["bq"][...] = _bf(qn)
    scr["bk"][...] = _bf(kn)
    scr["bbr"][...] = beta * jnp.where(cst["strict_w"][...] > 0, dec, 0.0)
    scr["brc"][...] = jnp.where(cst["causal_w"][...] > 0, dec, 0.0)
    scr["brhs"][...] = _bf(jnp.concatenate([beta * vv, beta * gam * kn], axis=1))
    scr["bqg"][...] = _bf(qn * gam)
    scr["bkt"][...] = _bf(kn * jnp.exp(glast - gc))
    scr["bcd"][...] = jnp.exp(glast)

    def gl_prep(q, k, v, logf, qx_ref, kx_ref, v_ref, cd_ref):
        hi, lo = _split(logf, 2)
        for j in range(2 * N_LEVELS + 2):
            sel = cst["sel"][j]
            e = jnp.exp(_dot(sel, hi) + _dot(sel, lo))
            if j < N_LEVELS:
                qx_ref[j] = _bf(q * e)
            elif j < 2 * N_LEVELS:
                kx_ref[j - N_LEVELS] = _bf(k * e)
            elif j == 2 * N_LEVELS:
                qx_ref[GL_EDGE] = _bf(q * e)
                cd_ref[...] = e
            else:
                kx_ref[GL_EDGE] = _bf(k * e)
        qx_ref[GL_DIAG] = _bf(q)
        kx_ref[GL_DIAG] = _bf(k)
        v_ref[...] = _bf(v)

    logit = _dot_hp(sm, par["gla_up"][...]) + par["gla_bias"][...]
    gl_prep(zcols(Z_CQ, GLA_KEYS) * (GLA_KEY_DIM ** -0.5),
            zcols(Z_CK, GLA_KEYS), zb_ref[:, Z_CV:Z_CV + GROUP],
            _log_sigmoid(logit) * (1.0 / GLA_GATE_NORM),
            scr["cqx"], scr["ckx"], scr["cv"], scr["ccd"])

    f_pre = zf_ref[:, F_DF:F_DF + GROUP]
    a = par["hg_loglb"][...]
    b = par["hg_log1mlb"][...] + _log_sigmoid(f_pre)
    logf_d = jnp.maximum(a, b) + jnp.log1p(jnp.exp(-jnp.abs(a - b)))
    gl_prep(zcols(Z_DQ, GROUP), par["hg_1mlb"][...] * jax.nn.sigmoid(-f_pre),
            zb_ref[:, Z_DI:Z_DI + GROUP], logf_d,
            scr["dqx"], scr["dkx"], scr["dv"], scr["dcd"])

    pa, pb, pc, pd = {}, {}, {}, {}

    def ret_p1(n):
        rs = rows[n]
        p = _dot_nt(scr["aq"][rs, :], _blockdiag(scr["ak"][rs, :], cst["bd_rope"][...]))
        yield
        pa[n] = _bf(p * cst["ret_d"][...])

    def gl_p1(n, qx_ref, kx_ref, bd_k, store):
        rs = rows[n]
        p = None
        for l in range(N_LEVELS + 1):
            t = _dot_nt(qx_ref[l, rs, :], _blockdiag(kx_ref[l, rs, :], bd_k[...]))
            yield
            t = t * cst["lvl"][l]
            p = t if p is None else p + t
        store[n] = _bf(p)

    def wide_matmul(a, b):
        return _dot(_bf(a), _blockdiag(_bf(b), bd_std[...]))

    def gdn_p1(n):
        rs = rows[n]
        knb = scr["bk"][rs, :]
        kbd = _blockdiag(knb, bd_std[...])
        kk = _dot_nt(knb, kbd)
        qk = _dot_nt(scr["bq"][rs, :], kbd)
        yield
        lmat = scr["bbr"][rs, :] * kk
        tinv = cst["eye_w"][...] - lmat * cst["lvl"][0]
        for li in range(1, N_LEVELS):
            xb = wide_matmul(lmat * cst["lvl"][li], tinv)
            yield
            tinv = tinv - wide_matmul(tinv, xb)
            yield
        uw = _dot(_bf(tinv), _blockdiag(scr["brhs"][rs, :], cst["bd_std2"][...]))
        yield
        pb[n] = (uw[:, 0:GROUP], _bf(uw[:, GROUP:2 * GROUP]), _bf(qk * scr["brc"][rs, :]))

    gens = []
    for n in range(n_chunks):
        gens += [gdn_p1(n), gl_p1(n, scr["cqx"], scr["ckx"], cst["bd_gla"], pc),
                 gl_p1(n, scr["dqx"], scr["dkx"], bd_std, pd), ret_p1(n)]
    _run_interleaved(gens)

    oscr = scr["oscr"]

    def ret_p2():
        s = scr["sa"][...]
        for n, rs in enumerate(rows):
            vbd = _blockdiag(scr["av"][rs, :], bd_std[...])
            o = _dot(pa[n], vbd) + _dot(scr["aqd"][rs, :], _bf(s))
            kv = _dot_tn(_blockdiag(scr["akd"][rs, :], cst["bd_rope"][...]), vbd)
            yield
            s = s * cst["ret_cdec"][...] + kv
            oscr[rs, 0:GROUP] = o
        scr["sa"][...] = s

    def gdn_p2():
        s = scr["sb"][...]
        for n, rs in enumerate(rows):
            u_v, w_k, qk = pb[n]
            r = _dot(jnp.concatenate([w_k, scr["bqg"][rs, :]], axis=0), _bf(s))
            yield
            u = u_v - r[0:CHUNK]
            ubd = _blockdiag(_bf(u), bd_std[...])
            o = r[CHUNK:2 * CHUNK] + _dot(qk, ubd)
            kv = _dot_tn(_blockdiag(scr["bkt"][rs, :], bd_std[...]), ubd)
            yield
            s = s * scr["bcd"][rs.start:rs.start + 1, :] + kv
            oscr[rs, GROUP:2 * GROUP] = o
        scr["sb"][...] = s

    def gl_p2(store, qx_ref, kx_ref, v_ref, cd_ref, bd_k, st_ref, col):
        st = st_ref[...]
        for n, rs in enumerate(rows):
            vbd = _blockdiag(v_ref[rs, :], bd_std[...])
            o = _dot(store[n], vbd) + _dot_nt(qx_ref[GL_EDGE, rs, :], _bf(st))
            kv = _dot_tn(vbd, _blockdiag(kx_ref[GL_EDGE, rs, :], bd_k[...]))
            yield
            st = st * cd_ref[rs.stop - 1:rs.stop, :] + kv
            oscr[rs, col:col + GROUP] = o
        st_ref[...] = st

    _run_interleaved([
        gdn_p2(), ret_p2(),
        gl_p2(pc, scr["cqx"], scr["ckx"], scr["cv"], scr["ccd"], cst["bd_gla"], scr["sct"],
              2 * GROUP),
        gl_p2(pd, scr["dqx"], scr["dkx"], scr["dv"], scr["dcd"], bd_std, scr["sdt"],
              3 * GROUP)])

    inv_hd = 1.0 / HEAD_DIM
    gains = (None, par["gdn_nw"], par["gla_nw"], par["hg_nw"])
    gates = (Z_RG, Z_BG, Z_CG, Z_DG)
    for i in range(4):
        cs = slice(i * GROUP, (i + 1) * GROUP)
        o = oscr[:, cs]
        o = o * lax.rsqrt(seg_sum(o * o) * inv_hd + EPS)
        if gains[i] is not None:
            o = o * gains[i][...]
        o_ref[:, cs] = _bf(o * _silu(zcols(gates[i], GROUP)))


def _mixer(zb, zf, cos_t, sin_t, params, batch, tb):
    m = zb.shape[0]
    t = m // batch
    nt = t // tb
    consts = {k: jnp.asarray(v, dt) for k, (v, dt) in _mixer_constants(tb).items()}
    const_args = [consts[k] for k in _CONST_ORDER]
    param_args = [params[k] for k in _PARAM_ORDER]

    def full_spec(a):
        nd = a.ndim
        return pl.BlockSpec(a.shape, lambda b, i, _nd=nd: (0,) * _nd)

    row_map = lambda b, i: (b * nt + i, 0)
    body = functools.partial(_mixer_body, n_chunks=tb // CHUNK)
    return pl.pallas_call(
        body,
        out_shape=jax.ShapeDtypeStruct((m, N_HEADS * GROUP), BF16),
        grid=(batch, nt),
        in_specs=[pl.BlockSpec((tb, Z_WIDTH), row_map),
                  pl.BlockSpec((tb, F_WIDTH), row_map),
                  pl.BlockSpec((tb, LANES), row_map),
                  pl.BlockSpec((tb, LANES), row_map)]
                 + [full_spec(a) for a in const_args]
                 + [full_spec(a) for a in param_args],
        out_specs=pl.BlockSpec((tb, N_HEADS * GROUP), row_map),
        scratch_shapes=_mixer_scratch(tb),
        compiler_params=pltpu.CompilerParams(
            dimension_semantics=("parallel", "arbitrary"),
            vmem_limit_bytes=VMEM_LIMIT),
        name="mixer",
    )(zb, zf, cos_t, sin_t, *const_args, *param_args)


def _in_proj_segments():
    src = {}
    off = 0
    for name, width in (("rq", 256), ("rk", 256), ("rv", 256), ("rg", 256),
                        ("bq", 256), ("bk", 256), ("bv", 256), ("ba", 4), ("bb", 4), ("bg", 256),
                        ("cq", 128), ("ck", 128), ("cv", 256), ("clr", 16), ("cg", 256),
                        ("dq", 256), ("df", 256), ("di", 256), ("dg", 256)):
        src[name] = off
        off += width

    def rope(name):
        return [(src[name] + h * HEAD_DIM + half * ROPE_HALF, ROPE_HALF)
                for half in range(2) for h in range(N_HEADS)]

    segs = rope("rq") + rope("rk")
    segs += [(src["rv"], 256), (src["rg"], 256), (src["bq"], 768), (src["bg"], 256),
             (src["cq"], 128), (src["ck"], 128), (src["cv"], 256), (src["cg"], 256),
             (src["dq"], 256), (src["di"], 256), (src["dg"], 256)]
    assert sum(w for _, w in segs) == Z_WIDTH
    segs += [(src["df"], 256), (src["ba"], 8), (src["clr"], GLA_LOWRANK),
             (None, LANES - SM_LR - GLA_LOWRANK)]
    assert sum(w for _, w in segs) == Z_WIDTH + F_WIDTH
    return segs


def kernel(x, mem, positions, mix_norm_w, w_in, gdn_conv_w, gdn_a_log, gdn_dt_bias, gdn_norm_w,
           gla_gk_up, gla_gk_bias, gla_norm_w, hgrn_lb_logits, hgrn_norm_w, w_out,
           xattn_norm_w, mem_norm_w, xattn_wq, xattn_wk, xattn_wv, xattn_wo,
           ffn_norm_w, ffn_up, ffn_conv_w, ffn_down, final_norm_w):
    batch, seq, d = x.shape
    depth = w_in.shape[0]
    m = batch * seq
    mem_len = mem.shape[1]

    w_in_p = jnp.concatenate(
        [jnp.zeros((depth, d, wd), BF16) if s0 is None else w_in[:, :, s0:s0 + wd].astype(BF16)
         for s0, wd in _in_proj_segments()], axis=2)

    freqs = ROPE_BASE ** (-jnp.arange(0, HEAD_DIM, 2, dtype=F32) / HEAD_DIM)
    freq_row = jnp.tile(freqs, N_HEADS).reshape(1, LANES)
    cos_t, sin_t = _rope_tables(positions.reshape(m, 1), freq_row, 1024)

    lb_all = jnp.cumsum(jax.nn.softmax(hgrn_lb_logits.astype(F32), axis=0), axis=0)
    lb_all = lb_all - lb_all[0]

    def head_row(v):
        return jnp.tile(v.astype(F32), N_HEADS).reshape(1, GROUP)

    h = x.reshape(m, d)
    mem2 = mem.reshape(batch * mem_len, d)
    for l in range(depth):
        gla_up = jnp.zeros((LANES, GLA_KEYS), F32).at[SM_LR:SM_LR + GLA_LOWRANK].set(gla_gk_up[l])
        params = {
            "gdn_conv": gdn_conv_w[l].astype(F32),
            "gdn_alog": jnp.repeat(gdn_a_log[l].astype(F32), HEAD_DIM).reshape(1, GROUP),
            "gdn_dt": jnp.repeat(gdn_dt_bias[l].astype(F32), HEAD_DIM).reshape(1, GROUP),
            "gdn_nw": head_row(gdn_norm_w[l]),
            "gla_up": gla_up,
            "gla_bias": gla_gk_bias[l].astype(F32).reshape(1, GLA_KEYS),
            "gla_nw": head_row(gla_norm_w[l]),
            "hg_loglb": jnp.log(lb_all[l]).reshape(1, GROUP),
            "hg_log1mlb": jnp.log1p(-lb_all[l]).reshape(1, GROUP),
            "hg_1mlb": (1.0 - lb_all[l]).reshape(1, GROUP),
            "hg_nw": head_row(hgrn_norm_w[l]),
        }
        zb, zf = _norm_matmul(h, mix_norm_w[l], w_in_p[l],
                              ((Z_WIDTH, BF16), (F_WIDTH, F32)), 512)
        o = _mixer(zb, zf, cos_t, sin_t, params, batch, 256)

        w_kv = jnp.concatenate([xattn_wk[l], xattn_wv[l]], axis=1).astype(BF16)
        kv, = _norm_matmul(mem2, mem_norm_w, w_kv, ((2 * d, BF16),), 512)
        h = _xattn(o, w_out[l].astype(BF16), h, xattn_norm_w[l], xattn_wq[l].astype(BF16),
                   kv.reshape(batch, mem_len, 2 * d), xattn_wo[l].astype(BF16), batch, 512)

        u, = _norm_matmul(h, ffn_norm_w[l], ffn_up[l].astype(BF16), ((ffn_up.shape[2], BF16),), 512)
        d_ff = ffn_down.shape[1]
        conv_w = ffn_conv_w[l].astype(F32) * jnp.where(jnp.arange(2 * d_ff) < d_ff, 0.5, 1.0)
        h = _conv_ffn_down(u, conv_w, ffn_down[l].astype(BF16), h,
                           final_norm_w, seq, 256, l == depth - 1)
    return h.reshape(batch, seq, d)
```

```python
import functools

import numpy as np
import jax
import jax.numpy as jnp
from jax import lax
from jax.experimental import pallas as pl
from jax.experimental.pallas import tpu as pltpu

F32 = jnp.float32
BF16 = jnp.bfloat16

N_HEADS = 4
HEAD_DIM = 64
GROUP = N_HEADS * HEAD_DIM
GLA_KEY_DIM = 32
GLA_KEYS = N_HEADS * GLA_KEY_DIM
GLA_LOWRANK = 16
GLA_GATE_NORM = 16.0
GDN_CONV = 4
CHUNK = 64
ROPE_BASE = 10000.0
ROPE_HALF = HEAD_DIM // 2
XATTN_HEADS = 4
FFN_CONV = 3
CONV_PHASES = 4
EPS = 1e-6

LANES = 128
VMEM_LIMIT = 52 * 1024 * 1024

Z_RQ, Z_RK, Z_RV, Z_RG = 0, 256, 512, 768
Z_BQKV, Z_BG = 1024, 1792
Z_CQ, Z_CK, Z_CV, Z_CG = 2048, 2176, 2304, 2560
Z_DQ, Z_DI, Z_DG = 2816, 3072, 3328
Z_WIDTH = 3584
F_DF, F_SMALL = 0, 256
F_WIDTH = 384
SM_BA, SM_BB, SM_LR = 0, 4, 8

LEVELS = (1, 2, 4, 8, 16, 32)
N_LEVELS = len(LEVELS)
GL_DIAG = N_LEVELS
GL_EDGE = N_LEVELS + 1


def _bf(x):
    return x.astype(BF16)


def _dot(a, b):
    return jnp.dot(a, b, preferred_element_type=F32)


def _dot_nt(a, b):
    return lax.dot_general(a, b, (((1,), (1,)), ((), ())), preferred_element_type=F32)


def _dot_tn(a, b):
    return lax.dot_general(a, b, (((0,), (0,)), ((), ())), preferred_element_type=F32)


def _split(x, n):
    parts = []
    r = x
    for i in range(n):
        p = r.astype(BF16)
        parts.append(p)
        if i < n - 1:
            r = r - p.astype(F32)
    return parts


def _sel_dot(c, x, n=2):
    acc = None
    for p in _split(x, n):
        t = _dot(c, p)
        acc = t if acc is None else acc + t
    return acc


def _dot_sel(x, c, n=2):
    acc = None
    for p in _split(x, n):
        t = _dot(p, c)
        acc = t if acc is None else acc + t
    return acc


def _dot_hp(a, b):
    a_hi, a_lo = _split(a, 2)
    b_hi, b_lo = _split(b, 2)
    return _dot(a_hi, b_hi) + _dot(a_lo, b_hi) + _dot(a_hi, b_lo)


def _silu(x):
    return x * jax.nn.sigmoid(x)


def _log_sigmoid(x):
    return jnp.minimum(x, 0.0) - jnp.log1p(jnp.exp(-jnp.abs(x)))


def _softplus(x):
    return jnp.maximum(x, 0.0) + jnp.log1p(jnp.exp(-jnp.abs(x)))


def _rms(x, w):
    ms = jnp.mean(x * x, axis=-1, keepdims=True)
    return x * lax.rsqrt(ms + EPS) * w


def _blockdiag(x, mask):
    return jnp.concatenate([x, x, x, x], axis=0) * mask


def _run_interleaved(gens):
    gens = list(gens)
    while gens:
        alive = []
        for g in gens:
            try:
                next(g)
                alive.append(g)
            except StopIteration:
                pass
        gens = alive


def _col_chunks(n, width):
    return [(c0, min(width, n - c0)) for c0 in range(0, n, width)]


def _norm_matmul_body(x_ref, nw_ref, w_ref, *o_refs, widths, col_chunk):
    xn = _bf(_rms(x_ref[...], nw_ref[...]))
    base = 0
    for o_ref, width in zip(o_refs, widths):
        for c0, cw in _col_chunks(width, col_chunk):
            o_ref[:, c0:c0 + cw] = _dot(xn, w_ref[:, base + c0:base + c0 + cw]).astype(o_ref.dtype)
        base += width


def _norm_matmul(x, nw, w, outs, tm):
    m, d = x.shape
    n = w.shape[1]
    widths = tuple(wd for wd, _ in outs)
    assert sum(widths) == n
    body = functools.partial(_norm_matmul_body, widths=widths, col_chunk=1024)
    res = pl.pallas_call(
        body,
        out_shape=tuple(jax.ShapeDtypeStruct((m, wd), dt) for wd, dt in outs),
        grid=(m // tm,),
        in_specs=[
            pl.BlockSpec((tm, d), lambda i: (i, 0)),
            pl.BlockSpec((1, d), lambda i: (0, 0)),
            pl.BlockSpec((d, n), lambda i: (0, 0), pipeline_mode=pl.Buffered(1)),
        ],
        out_specs=tuple(pl.BlockSpec((tm, wd), lambda i: (i, 0)) for wd, _ in outs),
        compiler_params=pltpu.CompilerParams(
            dimension_semantics=("parallel",),
            vmem_limit_bytes=VMEM_LIMIT),
        name="norm_matmul",
    )(x, nw.reshape(1, d), w)
    return res


def _rope_table_body(pos_ref, freq_ref, cos_ref, sin_ref):
    ang = pos_ref[...].astype(F32) * freq_ref[...]
    cos_ref[...] = jnp.cos(ang)
    sin_ref[...] = jnp.sin(ang)


def _rope_tables(pos, freq_row, tm):
    m = pos.shape[0]
    return pl.pallas_call(
        _rope_table_body,
        out_shape=(jax.ShapeDtypeStruct((m, LANES), F32),
                   jax.ShapeDtypeStruct((m, LANES), F32)),
        grid=(m // tm,),
        in_specs=[
            pl.BlockSpec((tm, 1), lambda i: (i, 0)),
            pl.BlockSpec((1, LANES), lambda i: (0, 0)),
        ],
        out_specs=(pl.BlockSpec((tm, LANES), lambda i: (i, 0)),
                   pl.BlockSpec((tm, LANES), lambda i: (i, 0))),
        compiler_params=pltpu.CompilerParams(dimension_semantics=("parallel",)),
        name="rope_tables",
    )(pos, freq_row)


def _xattn_body(mix_ref, wmix_ref, h_ref, nw_ref, wq_ref, k_ref, v_ref, wo_ref, o_ref, att_ref):
    h = h_ref[...] + _dot(mix_ref[...], wmix_ref[...])
    d = h.shape[-1]
    hd = d // XATTN_HEADS
    hn = _bf(_rms(h, nw_ref[...]))
    q = _bf(_dot(hn, wq_ref[...]) * (hd ** -0.5))
    for i in range(XATTN_HEADS):
        cs = slice(i * hd, (i + 1) * hd)
        s = _dot_nt(q[:, cs], k_ref[0, :, cs])
        s = s - jnp.max(s, axis=-1, keepdims=True)
        p = jnp.exp(s)
        p = p / jnp.sum(p, axis=-1, keepdims=True)
        att_ref[:, cs] = _bf(_dot(_bf(p), v_ref[0, :, cs]))
    o_ref[...] = h + _dot(att_ref[...], wo_ref[...])


def _xattn(mix, w_mix, h, nw, wq, kv, wo, batch, tq):
    m, d = h.shape
    t = m // batch
    mem_len = kv.shape[1]
    nt = t // tq
    row_map = lambda b, i: (b * nt + i, 0)
    const_map = lambda b, i: (0, 0)
    weight = lambda: pl.BlockSpec((d, d), const_map, pipeline_mode=pl.Buffered(1))
    return pl.pallas_call(
        _xattn_body,
        out_shape=jax.ShapeDtypeStruct((m, d), F32),
        grid=(batch, nt),
        in_specs=[
            pl.BlockSpec((tq, d), row_map),
            weight(),
            pl.BlockSpec((tq, d), row_map),
            pl.BlockSpec((1, d), const_map),
            weight(),
            pl.BlockSpec((1, mem_len, d), lambda b, i: (b, 0, 0)),
            pl.BlockSpec((1, mem_len, d), lambda b, i: (b, 0, 1)),
            weight(),
        ],
        out_specs=pl.BlockSpec((tq, d), row_map),
        scratch_shapes=[pltpu.VMEM((tq, d), BF16)],
        compiler_params=pltpu.CompilerParams(
            dimension_semantics=("parallel", "parallel"),
            vmem_limit_bytes=VMEM_LIMIT),
        name="xattn",
    )(mix, w_mix, h, nw.reshape(1, d), wq, kv, kv, wo)


def _shift_rows(x, k, tail, row):
    y = pltpu.roll(x, k, 0)
    head = y[0:8]
    for j in range(k):
        src = tail.shape[0] - k + j
        head = jnp.where(row[0:8] == j, tail[src:src + 1, :], head)
    return jnp.concatenate([head, y[8:]], axis=0)


def _conv_ffn_down_body(u_ref, halo_ref, cw_ref, wd_ref, r_ref, fw_ref, o_ref,
                        ext_ref, yst_ref, act_ref, *, tiles_per_seq, final_norm):
    tm = u_ref.shape[0]
    d_ff = act_ref.shape[1]
    first = (pl.program_id(0) % tiles_per_seq) == 0
    keep = jnp.where(first, 0.0, 1.0)
    hrows = halo_ref.shape[0]
    nq = tm // CONV_PHASES

    def phase_rows(half, off):
        return ext_ref[half, pl.ds(8 + off, nq, stride=CONV_PHASES), :]

    for c0 in range(0, d_ff, LANES):
        for half in range(2):
            cs = slice(half * d_ff + c0, half * d_ff + c0 + LANES)
            ext_ref[half, 0:8, :] = halo_ref[hrows - 8:hrows, cs].astype(F32) * keep
            ext_ref[half, 8:8 + tm, :] = u_ref[:, cs].astype(F32)
        xs = [[phase_rows(half, off) for off in range(1 - FFN_CONV, CONV_PHASES)]
              for half in range(2)]
        ws = [cw_ref[:, half * d_ff + c0:half * d_ff + c0 + LANES] for half in range(2)]
        for p in range(CONV_PHASES):
            ys = []
            for half in range(2):
                y = None
                for k in range(FFN_CONV):
                    t = xs[half][p + k] * ws[half][k:k + 1, :]
                    y = t if y is None else y + t
                ys.append(y)
            hg = ys[0]
            yst_ref[pl.ds(p, nq, stride=CONV_PHASES), :] = (hg + hg * jnp.tanh(hg)) * ys[1]
        act_ref[:, c0:c0 + LANES] = _bf(yst_ref[...])
    out = r_ref[...] + _dot(act_ref[...], wd_ref[...])
    if final_norm:
        out = _rms(out, fw_ref[...])
    o_ref[...] = out


def _conv_ffn_down(u, conv_w, w_down, res, final_w, seq_len, tm, final_norm):
    m, n2 = u.shape
    d_ff = n2 // 2
    d = w_down.shape[1]
    halo = 16
    body = functools.partial(_conv_ffn_down_body, tiles_per_seq=seq_len // tm,
                             final_norm=final_norm)
    return pl.pallas_call(
        body,
        out_shape=jax.ShapeDtypeStruct((m, d), F32),
        grid=(m // tm,),
        in_specs=[
            pl.BlockSpec((tm, n2), lambda i: (i, 0)),
            pl.BlockSpec((halo, n2), lambda i: (jnp.maximum(i * (tm // halo) - 1, 0), 0)),
            pl.BlockSpec((FFN_CONV, n2), lambda i: (0, 0)),
            pl.BlockSpec((d_ff, d), lambda i: (0, 0), pipeline_mode=pl.Buffered(1)),
            pl.BlockSpec((tm, d), lambda i: (i, 0)),
            pl.BlockSpec((1, d), lambda i: (0, 0)),
        ],
        out_specs=pl.BlockSpec((tm, d), lambda i: (i, 0)),
        scratch_shapes=[pltpu.VMEM((2, 8 + tm, LANES), F32),
                        pltpu.VMEM((tm, LANES), F32),
                        pltpu.VMEM((tm, d_ff), BF16)],
        compiler_params=pltpu.CompilerParams(
            dimension_semantics=("parallel",),
            vmem_limit_bytes=VMEM_LIMIT),
        name="conv_ffn_down",
    )(u, u, conv_w, w_down, res, final_w.reshape(1, d))


def _mixer_constants(tb):
    c = CHUNK
    ar = np.arange
    std_head = ar(GROUP) // HEAD_DIM
    rope_head = (ar(GROUP) % LANES) // ROPE_HALF
    gla_head = ar(GLA_KEYS) // GLA_KEY_DIM
    row_head = ar(GROUP) // c
    s_of = ar(GROUP) % c
    t = ar(c)
    tt = ar(tb) % c
    same_chunk = (ar(tb)[:, None] // c) == (ar(tb)[None, :] // c)

    out = {}
    bd_std = row_head[:, None] == std_head[None, :]
    out["bd_std"] = (bd_std, BF16)
    out["bd_std2"] = (np.concatenate([bd_std, bd_std], axis=1), BF16)
    out["bd_rope"] = (row_head[:, None] == rope_head[None, :], BF16)
    out["bd_gla"] = (row_head[:, None] == gla_head[None, :], BF16)

    log_gamma = np.log(1.0 - np.exp2(-5.0 - ar(N_HEADS, dtype=np.float64)))
    scale = HEAD_DIM ** -0.5
    rel = t[:, None] - s_of[None, :]
    lg_w = log_gamma[std_head][None, :]
    out["ret_d"] = (np.where(rel >= 0, np.exp(lg_w * np.maximum(rel, 0)), 0.0) * scale, F32)
    lg_r = log_gamma[rope_head][None, :]
    out["ret_qdec"] = (np.exp(lg_r * (tt[:, None] + 1.0)), F32)
    out["ret_kdec"] = (np.exp(lg_r * (c - 1.0 - tt[:, None])) * scale, F32)
    out["ret_cdec"] = (np.exp(log_gamma[std_head] * c)[None, :], F32)

    lvl, sel_q, sel_k = [], [], []
    for b in LEVELS:
        blk = t // (2 * b)
        second = (t % (2 * b)) >= b
        bound = blk * 2 * b + b - 1
        msk = (blk[:, None] == blk[None, :]) & second[:, None] & (~second)[None, :]
        lvl.append(msk[:, s_of])
        sel_q.append(second[:, None] & (t[None, :] > bound[:, None]) & (t[None, :] <= t[:, None]))
        sel_k.append((~second)[:, None] & (t[None, :] > t[:, None]) & (t[None, :] <= bound[:, None]))
    lvl.append(t[:, None] == s_of[None, :])
    out["lvl"] = (np.stack(lvl), F32)
    tri = t[None, :] <= t[:, None]
    tail = t[None, :] > t[:, None]
    eye_b = np.eye(tb // c, dtype=bool)
    sel = [np.kron(eye_b, s_) for s_ in [a | b for a, b in zip(sel_q, sel_k)] + [tri, tail]]
    out["sel"] = (np.stack(sel), BF16)
    out["ones_b"] = (same_chunk, BF16)
    out["trit_w"] = (tt[:, None] <= s_of[None, :], F32)
    out["causal_w"] = (tt[:, None] >= s_of[None, :], F32)
    out["strict_w"] = (tt[:, None] > s_of[None, :], F32)
    out["eye_w"] = (t[:, None] == s_of[None, :], F32)
    eab = np.zeros((LANES, 2 * GROUP))
    eab[SM_BA + std_head, ar(GROUP)] = 1.0
    eab[SM_BB + std_head, GROUP + ar(GROUP)] = 1.0
    out["exp_ab"] = (eab, BF16)
    return out


_CONST_ORDER = ("bd_std", "bd_std2", "bd_rope", "bd_gla", "ret_d", "ret_qdec", "ret_kdec",
                "ret_cdec", "lvl", "sel", "ones_b", "trit_w", "causal_w", "strict_w", "eye_w",
                "exp_ab")
_PARAM_ORDER = ("gdn_conv", "gdn_alog", "gdn_dt", "gdn_nw", "gla_up", "gla_bias", "gla_nw",
                "hg_loglb", "hg_log1mlb", "hg_1mlb", "hg_nw")
_SCRATCH_ORDER = ("sa", "sb", "sct", "sdt", "tail",
                  "aq", "aqd", "ak", "akd", "av",
                  "bq", "bk", "bbr", "brc", "brhs", "bqg", "bkt", "bcd",
                  "cqx", "ckx", "cv", "ccd", "dqx", "dkx", "dv", "dcd", "oscr")


def _mixer_scratch(tb):
    f, b = F32, BF16
    shapes = {
        "sa": ((GROUP, GROUP), f), "sb": ((GROUP, GROUP), f),
        "sct": ((GROUP, GLA_KEYS), f), "sdt": ((GROUP, GROUP), f),
        "tail": ((8, 3 * GROUP), f),
        "aq": ((tb, GROUP), b), "aqd": ((tb, GROUP), b), "ak": ((tb, GROUP), b),
        "akd": ((tb, GROUP), b), "av": ((tb, GROUP), b),
        "bq": ((tb, GROUP), b), "bk": ((tb, GROUP), b), "bbr": ((tb, GROUP), f),
        "brc": ((tb, GROUP), f), "brhs": ((tb, 2 * GROUP), b), "bqg": ((tb, GROUP), b),
        "bkt": ((tb, GROUP), b), "bcd": ((tb, GROUP), f),
        "cqx": ((N_LEVELS + 2, tb, GLA_KEYS), b), "ckx": ((N_LEVELS + 2, tb, GLA_KEYS), b),
        "cv": ((tb, GROUP), b), "ccd": ((tb, GLA_KEYS), f),
        "dqx": ((N_LEVELS + 2, tb, GROUP), b), "dkx": ((N_LEVELS + 2, tb, GROUP), b),
        "dv": ((tb, GROUP), b), "dcd": ((tb, GROUP), f),
        "oscr": ((tb, N_HEADS * GROUP), f),
    }
    return [pltpu.VMEM(*shapes[k]) for k in _SCRATCH_ORDER]


def _mixer_body(*refs, n_chunks):
    zb_ref, zf_ref, cos_ref, sin_ref = refs[:4]
    nc, npar = len(_CONST_ORDER), len(_PARAM_ORDER)
    cst = dict(zip(_CONST_ORDER, refs[4:4 + nc]))
    par = dict(zip(_PARAM_ORDER, refs[4 + nc:4 + nc + npar]))
    o_ref = refs[4 + nc + npar]
    scr = dict(zip(_SCRATCH_ORDER, refs[5 + nc + npar:]))
    tb = zb_ref.shape[0]
    rows = [slice(n * CHUNK, (n + 1) * CHUNK) for n in range(n_chunks)]
    bd_std = cst["bd_std"]

    def zcols(col, width):
        return zb_ref[:, col:col + width].astype(F32)

    @pl.when(pl.program_id(1) == 0)
    def _():
        for k in ("sa", "sb", "sct", "sdt", "tail"):
            scr[k][...] = jnp.zeros_like(scr[k])

    def seg_sum(x):
        return _dot_sel(x, bd_std[...])

    cos = cos_ref[...]
    sin = sin_ref[...]

    def rope(col):
        x1 = zcols(col, LANES)
        x2 = zcols(col + LANES, LANES)
        return jnp.concatenate([x1 * cos - x2 * sin, x1 * sin + x2 * cos], axis=1)

    qr = rope(Z_RQ)
    scr["aq"][...] = _bf(qr)
    scr["aqd"][...] = _bf(qr * cst["ret_qdec"][...])
    kr = rope(Z_RK)
    scr["ak"][...] = _bf(kr)
    scr["akd"][...] = _bf(kr * cst["ret_kdec"][...])
    scr["av"][...] = zb_ref[:, Z_RV:Z_RV + GROUP]

    row = lax.broadcasted_iota(jnp.int32, (tb, 1), 0)
    x = zcols(Z_BQKV, 3 * GROUP)
    tail = scr["tail"][...]
    cw = par["gdn_conv"][...]
    y = x * cw[GDN_CONV - 1:GDN_CONV, :]
    for k in range(1, GDN_CONV):
        y = y + _shift_rows(x, k, tail, row) * cw[GDN_CONV - 1 - k:GDN_CONV - k, :]
    scr["tail"][...] = x[tb - 8:tb, :]
    qkv = _silu(y)
    qb, kb, vv = qkv[:, 0:GROUP], qkv[:, GROUP:2 * GROUP], qkv[:, 2 * GROUP:3 * GROUP]
    sm = zf_ref[:, F_SMALL:F_SMALL + LANES]
    bab = _dot_sel(sm, cst["exp_ab"][...])
    g = -jnp.exp(par["gdn_alog"][...]) * _softplus(bab[:, 0:GROUP] + par["gdn_dt"][...])
    beta = jax.nn.sigmoid(bab[:, GROUP:2 * GROUP])
    qn = qb * lax.rsqrt(seg_sum(qb * qb) + EPS) * (HEAD_DIM ** -0.5)
    kn = kb * lax.rsqrt(seg_sum(kb * kb) + EPS)
    ones_b = cst["ones_b"][...]
    gc = _sel_dot(cst["sel"][N_LEVELS], g)
    gcr = _sel_dot(ones_b, g * cst["trit_w"][...])
    glast = _sel_dot(ones_b, g)
    dec = jnp.exp(gc - gcr)
    gam = jnp.exp(gc)
    scr["bq"][...] = _bf(qn)
    scr["bk"][...] = _bf(kn)
    scr["bbr"][...] = beta * jnp.where(cst["strict_w"][...] > 0, dec, 0.0)
    scr["brc"][...] = jnp.where(cst["causal_w"][...] > 0, dec, 0.0)
    scr["brhs"][...] = _bf(jnp.concatenate([beta * vv, beta * gam * kn], axis=1))
    scr["bqg"][...] = _bf(qn * gam)
    scr["bkt"][...] = _bf(kn * jnp.exp(glast - gc))
    scr["bcd"][...] = jnp.exp(glast)

    def gl_prep(q, k, v, logf, qx_ref, kx_ref, v_ref, cd_ref):
        hi, lo = _split(logf, 2)
        for j in range(N_LEVELS + 2):
            sel = cst["sel"][j]
            e = jnp.exp(_dot(sel, hi) + _dot(sel, lo))
            if j < N_LEVELS:
                qx_ref[j] = _bf(q * e)
                kx_ref[j] = _bf(k * e)
            elif j == N_LEVELS:
                qx_ref[GL_EDGE] = _bf(q * e)
                cd_ref[...] = e
            else:
                kx_ref[GL_EDGE] = _bf(k * e)
            yield
        qx_ref[GL_DIAG] = _bf(q)
        kx_ref[GL_DIAG] = _bf(k)
        v_ref[...] = _bf(v)

    logit = _dot_hp(sm, par["gla_up"][...]) + par["gla_bias"][...]
    prep_c = gl_prep(zcols(Z_CQ, GLA_KEYS) * (GLA_KEY_DIM ** -0.5),
                     zcols(Z_CK, GLA_KEYS), zb_ref[:, Z_CV:Z_CV + GROUP],
                     _log_sigmoid(logit) * (1.0 / GLA_GATE_NORM),
                     scr["cqx"], scr["ckx"], scr["cv"], scr["ccd"])

    f_pre = zf_ref[:, F_DF:F_DF + GROUP]
    a = par["hg_loglb"][...]
    b = par["hg_log1mlb"][...] + _log_sigmoid(f_pre)
    logf_d = jnp.maximum(a, b) + jnp.log1p(jnp.exp(-jnp.abs(a - b)))
    prep_d = gl_prep(zcols(Z_DQ, GROUP), par["hg_1mlb"][...] * jax.nn.sigmoid(-f_pre),
                     zb_ref[:, Z_DI:Z_DI + GROUP], logf_d,
                     scr["dqx"], scr["dkx"], scr["dv"], scr["dcd"])

    pa, pb, pc, pd = {}, {}, {}, {}

    def ret_p1(n):
        rs = rows[n]
        p = _dot_nt(scr["aq"][rs, :], _blockdiag(scr["ak"][rs, :], cst["bd_rope"][...]))
        yield
        pa[n] = _bf(p * cst["ret_d"][...])

    def gl_p1(n, qx_ref, kx_ref, bd_k, store):
        rs = rows[n]
        p = None
        for l in range(N_LEVELS + 1):
            t = _dot_nt(qx_ref[l, rs, :], _blockdiag(kx_ref[l, rs, :], bd_k[...]))
            yield
            t = t * cst["lvl"][l]
            p = t if p is None else p + t
        store[n] = _bf(p)

    def wide_matmul(a, b):
        return _dot(_bf(a), _blockdiag(_bf(b), bd_std[...]))

    def gdn_p1(n):
        rs = rows[n]
        knb = scr["bk"][rs, :]
        kbd = _blockdiag(knb, bd_std[...])
        kk = _dot_nt(knb, kbd)
        qk = _dot_nt(scr["bq"][rs, :], kbd)
        yield
        lmat = scr["bbr"][rs, :] * kk
        tinv = cst["eye_w"][...] - lmat * cst["lvl"][0]
        for li in range(1, N_LEVELS):
            xb = wide_matmul(lmat * cst["lvl"][li], tinv)
            yield
            tinv = tinv - wide_matmul(tinv, xb)
            yield
        uw = _dot(_bf(tinv), _blockdiag(scr["brhs"][rs, :], cst["bd_std2"][...]))
        yield
        pb[n] = (uw[:, 0:GROUP], _bf(uw[:, GROUP:2 * GROUP]), _bf(qk * scr["brc"][rs, :]))


    oscr = scr["oscr"]

    def ret_p2():
        s = scr["sa"][...]
        for n, rs in enumerate(rows):
            vbd = _blockdiag(scr["av"][rs, :], bd_std[...])
            o = _dot(pa[n], vbd) + _dot(scr["aqd"][rs, :], _bf(s))
            kv = _dot_tn(_blockdiag(scr["akd"][rs, :], cst["bd_rope"][...]), vbd)
            yield
            s = s * cst["ret_cdec"][...] + kv
            oscr[rs, 0:GROUP] = o
        scr["sa"][...] = s

    def gdn_p2():
        s = scr["sb"][...]
        for n, rs in enumerate(rows):
            u_v, w_k, qk = pb[n]
            r = _dot(jnp.concatenate([w_k, scr["bqg"][rs, :]], axis=0), _bf(s))
            yield
            u = u_v - r[0:CHUNK]
            ubd = _blockdiag(_bf(u), bd_std[...])
            o = r[CHUNK:2 * CHUNK] + _dot(qk, ubd)
            kv = _dot_tn(_blockdiag(scr["bkt"][rs, :], bd_std[...]), ubd)
            yield
            s = s * scr["bcd"][rs.start:rs.start + 1, :] + kv
            oscr[rs, GROUP:2 * GROUP] = o
        scr["sb"][...] = s

    def gl_p2(store, qx_ref, kx_ref, v_ref, cd_ref, bd_k, st_ref, col):
        st = st_ref[...]
        for n, rs in enumerate(rows):
            vbd = _blockdiag(v_ref[rs, :], bd_std[...])
            o = _dot(store[n], vbd) + _dot_nt(qx_ref[GL_EDGE, rs, :], _bf(st))
            kv = _dot_tn(vbd, _blockdiag(kx_ref[GL_EDGE, rs, :], bd_k[...]))
            yield
            st = st * cd_ref[rs.stop - 1:rs.stop, :] + kv
            oscr[rs, col:col + GROUP] = o
        st_ref[...] = st

    def finish(idx):
        inv_hd = 1.0 / HEAD_DIM
        gains = (None, par["gdn_nw"], par["gla_nw"], par["hg_nw"])
        gates = (Z_RG, Z_BG, Z_CG, Z_DG)
        for i in idx:
            o = oscr[:, i * GROUP:(i + 1) * GROUP]
            o = o * lax.rsqrt(seg_sum(o * o) * inv_hd + EPS)
            yield
            if gains[i] is not None:
                o = o * gains[i][...]
            o_ref[:, i * GROUP:(i + 1) * GROUP] = _bf(o * _silu(zcols(gates[i], GROUP)))

    _run_interleaved([g for n in range(n_chunks) for g in (gdn_p1(n), ret_p1(n))]
                     + [prep_c, prep_d])
    _run_interleaved([g for n in range(n_chunks)
                      for g in (gl_p1(n, scr["cqx"], scr["ckx"], cst["bd_gla"], pc),
                                gl_p1(n, scr["dqx"], scr["dkx"], bd_std, pd))]
                     + [gdn_p2(), ret_p2()])
    _run_interleaved([
        gl_p2(pc, scr["cqx"], scr["ckx"], scr["cv"], scr["ccd"], cst["bd_gla"], scr["sct"],
              2 * GROUP),
        gl_p2(pd, scr["dqx"], scr["dkx"], scr["dv"], scr["dcd"], bd_std, scr["sdt"],
              3 * GROUP),
        finish((0, 1))])
    _run_interleaved([finish((2, 3))])


def _mixer(zb, zf, cos_t, sin_t, params, batch, tb):
    m = zb.shape[0]
    t = m // batch
    nt = t // tb
    consts = {k: jnp.asarray(v, dt) for k, (v, dt) in _mixer_constants(tb).items()}
    const_args = [consts[k] for k in _CONST_ORDER]
    param_args = [params[k] for k in _PARAM_ORDER]

    def full_spec(a):
        nd = a.ndim
        return pl.BlockSpec(a.shape, lambda b, i, _nd=nd: (0,) * _nd)

    row_map = lambda b, i: (b * nt + i, 0)
    body = functools.partial(_mixer_body, n_chunks=tb // CHUNK)
    return pl.pallas_call(
        body,
        out_shape=jax.ShapeDtypeStruct((m, N_HEADS * GROUP), BF16),
        grid=(batch, nt),
        in_specs=[pl.BlockSpec((tb, Z_WIDTH), row_map),
                  pl.BlockSpec((tb, F_WIDTH), row_map),
                  pl.BlockSpec((tb, LANES), row_map),
                  pl.BlockSpec((tb, LANES), row_map)]
                 + [full_spec(a) for a in const_args]
                 + [full_spec(a) for a in param_args],
        out_specs=pl.BlockSpec((tb, N_HEADS * GROUP), row_map),
        scratch_shapes=_mixer_scratch(tb),
        compiler_params=pltpu.CompilerParams(
            dimension_semantics=("parallel", "arbitrary"),
            vmem_limit_bytes=VMEM_LIMIT),
        name="mixer",
    )(zb, zf, cos_t, sin_t, *const_args, *param_args)


def _in_proj_segments():
    src = {}
    off = 0
    for name, width in (("rq", 256), ("rk", 256), ("rv", 256), ("rg", 256),
                        ("bq", 256), ("bk", 256), ("bv", 256), ("ba", 4), ("bb", 4), ("bg", 256),
                        ("cq", 128), ("ck", 128), ("cv", 256), ("clr", 16), ("cg", 256),
                        ("dq", 256), ("df", 256), ("di", 256), ("dg", 256)):
        src[name] = off
        off += width

    def rope(name):
        return [(src[name] + h * HEAD_DIM + half * ROPE_HALF, ROPE_HALF)
                for half in range(2) for h in range(N_HEADS)]

    segs = rope("rq") + rope("rk")
    segs += [(src["rv"], 256), (src["rg"], 256), (src["bq"], 768), (src["bg"], 256),
             (src["cq"], 128), (src["ck"], 128), (src["cv"], 256), (src["cg"], 256),
             (src["dq"], 256), (src["di"], 256), (src["dg"], 256)]
    assert sum(w for _, w in segs) == Z_WIDTH
    segs += [(src["df"], 256), (src["ba"], 8), (src["clr"], GLA_LOWRANK),
             (None, LANES - SM_LR - GLA_LOWRANK)]
    assert sum(w for _, w in segs) == Z_WIDTH + F_WIDTH
    return segs


def kernel(x, mem, positions, mix_norm_w, w_in, gdn_conv_w, gdn_a_log, gdn_dt_bias, gdn_norm_w,
           gla_gk_up, gla_gk_bias, gla_norm_w, hgrn_lb_logits, hgrn_norm_w, w_out,
           xattn_norm_w, mem_norm_w, xattn_wq, xattn_wk, xattn_wv, xattn_wo,
           ffn_norm_w, ffn_up, ffn_conv_w, ffn_down, final_norm_w):
    batch, seq, d = x.shape
    depth = w_in.shape[0]
    m = batch * seq
    mem_len = mem.shape[1]

    w_in_p = jnp.concatenate(
        [jnp.zeros((depth, d, wd), BF16) if s0 is None else w_in[:, :, s0:s0 + wd].astype(BF16)
         for s0, wd in _in_proj_segments()], axis=2)

    freqs = ROPE_BASE ** (-jnp.arange(0, HEAD_DIM, 2, dtype=F32) / HEAD_DIM)
    freq_row = jnp.tile(freqs, N_HEADS).reshape(1, LANES)
    cos_t, sin_t = _rope_tables(positions.reshape(m, 1), freq_row, 1024)

    lb_all = jnp.cumsum(jax.nn.softmax(hgrn_lb_logits.astype(F32), axis=0), axis=0)
    lb_all = lb_all - lb_all[0]

    def head_row(v):
        return jnp.tile(v.astype(F32), N_HEADS).reshape(1, GROUP)

    h = x.reshape(m, d)
    mem2 = mem.reshape(batch * mem_len, d)
    for l in range(depth):
        gla_up = jnp.zeros((LANES, GLA_KEYS), F32).at[SM_LR:SM_LR + GLA_LOWRANK].set(gla_gk_up[l])
        params = {
            "gdn_conv": gdn_conv_w[l].astype(F32),
            "gdn_alog": jnp.repeat(gdn_a_log[l].astype(F32), HEAD_DIM).reshape(1, GROUP),
            "gdn_dt": jnp.repeat(gdn_dt_bias[l].astype(F32), HEAD_DIM).reshape(1, GROUP),
            "gdn_nw": head_row(gdn_norm_w[l]),
            "gla_up": gla_up,
            "gla_bias": gla_gk_bias[l].astype(F32).reshape(1, GLA_KEYS),
            "gla_nw": head_row(gla_norm_w[l]),
            "hg_loglb": jnp.log(lb_all[l]).reshape(1, GROUP),
            "hg_log1mlb": jnp.log1p(-lb_all[l]).reshape(1, GROUP),
            "hg_1mlb": (1.0 - lb_all[l]).reshape(1, GROUP),
            "hg_nw": head_row(hgrn_norm_w[l]),
        }
        zb, zf = _norm_matmul(h, mix_norm_w[l], w_in_p[l],
                              ((Z_WIDTH, BF16), (F_WIDTH, F32)), 512)
        o = _mixer(zb, zf, cos_t, sin_t, params, batch, 256)

        w_kv = jnp.concatenate([xattn_wk[l], xattn_wv[l]], axis=1).astype(BF16)
        kv, = _norm_matmul(mem2, mem_norm_w, w_kv, ((2 * d, BF16),), 512)
        h = _xattn(o, w_out[l].astype(BF16), h, xattn_norm_w[l], xattn_wq[l].astype(BF16),
                   kv.reshape(batch, mem_len, 2 * d), xattn_wo[l].astype(BF16), batch, 512)

        u, = _norm_matmul(h, ffn_norm_w[l], ffn_up[l].astype(BF16), ((ffn_up.shape[2], BF16),), 512)
        d_ff = ffn_down.shape[1]
        conv_w = ffn_conv_w[l].astype(F32) * jnp.where(jnp.arange(2 * d_ff) < d_ff, 0.5, 1.0)
        h = _conv_ffn_down(u, conv_w, ffn_down[l].astype(BF16), h,
                           final_norm_w, seq, 256, l == depth - 1)
    return h.reshape(batch, seq, d)
```

```python
import functools

import numpy as np
import jax
import jax.numpy as jnp
from jax import lax
from jax.experimental import pallas as pl
from jax.experimental.pallas import tpu as pltpu

F32 = jnp.float32
BF16 = jnp.bfloat16

N_HEADS = 4
HEAD_DIM = 64
GROUP = N_HEADS * HEAD_DIM
GLA_KEY_DIM = 32
GLA_KEYS = N_HEADS * GLA_KEY_DIM
GLA_LOWRANK = 16
GLA_GATE_NORM = 16.0
GDN_CONV = 4
CHUNK = 64
ROPE_BASE = 10000.0
ROPE_HALF = HEAD_DIM // 2
XATTN_HEADS = 4
FFN_CONV = 3
CONV_PHASES = 4
EPS = 1e-6

LANES = 128
VMEM_LIMIT = 52 * 1024 * 1024

Z_RQ, Z_RK, Z_RV, Z_RG = 0, 256, 512, 768
Z_BQKV, Z_BG = 1024, 1792
Z_CQ, Z_CK, Z_CV, Z_CG = 2048, 2176, 2304, 2560
Z_DQ, Z_DI, Z_DG = 2816, 3072, 3328
Z_WIDTH = 3584
F_DF, F_SMALL = 0, 256
F_WIDTH = 384
SM_BA, SM_BB, SM_LR = 0, 4, 8

LEVELS = (1, 2, 4, 8, 16, 32)
N_LEVELS = len(LEVELS)
GL_DIAG = N_LEVELS
GL_EDGE = N_LEVELS + 1


def _bf(x):
    return x.astype(BF16)


def _dot(a, b):
    return jnp.dot(a, b, preferred_element_type=F32)


def _dot_nt(a, b):
    return lax.dot_general(a, b, (((1,), (1,)), ((), ())), preferred_element_type=F32)


def _dot_tn(a, b):
    return lax.dot_general(a, b, (((0,), (0,)), ((), ())), preferred_element_type=F32)


def _split(x, n):
    parts = []
    r = x
    for i in range(n):
        p = r.astype(BF16)
        parts.append(p)
        if i < n - 1:
            r = r - p.astype(F32)
    return parts


def _sel_dot(c, x, n=2):
    acc = None
    for p in _split(x, n):
        t = _dot(c, p)
        acc = t if acc is None else acc + t
    return acc


def _dot_sel(x, c, n=2):
    acc = None
    for p in _split(x, n):
        t = _dot(p, c)
        acc = t if acc is None else acc + t
    return acc


def _dot_hp(a, b):
    a_hi, a_lo = _split(a, 2)
    b_hi, b_lo = _split(b, 2)
    return _dot(a_hi, b_hi) + _dot(a_lo, b_hi) + _dot(a_hi, b_lo)


def _silu(x):
    return x * jax.nn.sigmoid(x)


def _log_sigmoid(x):
    return jnp.minimum(x, 0.0) - jnp.log1p(jnp.exp(-jnp.abs(x)))


def _softplus(x):
    return jnp.maximum(x, 0.0) + jnp.log1p(jnp.exp(-jnp.abs(x)))


def _rms(x, w):
    ms = jnp.mean(x * x, axis=-1, keepdims=True)
    return x * lax.rsqrt(ms + EPS) * w


def _run_interleaved(gens):
    gens = list(gens)
    while gens:
        alive = []
        for g in gens:
            try:
                next(g)
                alive.append(g)
            except StopIteration:
                pass
        gens = alive


def _col_chunks(n, width):
    return [(c0, min(width, n - c0)) for c0 in range(0, n, width)]


def _norm_matmul_body(x_ref, nw_ref, w_ref, *o_refs, widths, col_chunk):
    xn = _bf(_rms(x_ref[...], nw_ref[...]))
    base = 0
    for o_ref, width in zip(o_refs, widths):
        for c0, cw in _col_chunks(width, col_chunk):
            o_ref[:, c0:c0 + cw] = _dot(xn, w_ref[:, base + c0:base + c0 + cw]).astype(o_ref.dtype)
        base += width


def _norm_matmul(x, nw, w, outs, tm):
    m, d = x.shape
    n = w.shape[1]
    widths = tuple(wd for wd, _ in outs)
    assert sum(widths) == n
    body = functools.partial(_norm_matmul_body, widths=widths, col_chunk=1024)
    res = pl.pallas_call(
        body,
        out_shape=tuple(jax.ShapeDtypeStruct((m, wd), dt) for wd, dt in outs),
        grid=(m // tm,),
        in_specs=[
            pl.BlockSpec((tm, d), lambda i: (i, 0)),
            pl.BlockSpec((1, d), lambda i: (0, 0)),
            pl.BlockSpec((d, n), lambda i: (0, 0), pipeline_mode=pl.Buffered(1)),
        ],
        out_specs=tuple(pl.BlockSpec((tm, wd), lambda i: (i, 0)) for wd, _ in outs),
        compiler_params=pltpu.CompilerParams(
            dimension_semantics=("parallel",),
            vmem_limit_bytes=VMEM_LIMIT),
        name="norm_matmul",
    )(x, nw.reshape(1, d), w)
    return res


def _rope_table_body(pos_ref, freq_ref, cos_ref, sin_ref):
    ang = pos_ref[...].astype(F32) * freq_ref[...]
    cos_ref[...] = jnp.cos(ang)
    sin_ref[...] = jnp.sin(ang)


def _rope_tables(pos, freq_row, tm):
    m = pos.shape[0]
    return pl.pallas_call(
        _rope_table_body,
        out_shape=(jax.ShapeDtypeStruct((m, LANES), F32),
                   jax.ShapeDtypeStruct((m, LANES), F32)),
        grid=(m // tm,),
        in_specs=[
            pl.BlockSpec((tm, 1), lambda i: (i, 0)),
            pl.BlockSpec((1, LANES), lambda i: (0, 0)),
        ],
        out_specs=(pl.BlockSpec((tm, LANES), lambda i: (i, 0)),
                   pl.BlockSpec((tm, LANES), lambda i: (i, 0))),
        compiler_params=pltpu.CompilerParams(dimension_semantics=("parallel",)),
        name="rope_tables",
    )(pos, freq_row)


def _xattn_body(mix_ref, wmix_ref, h_ref, nw_ref, wq_ref, k_ref, v_ref, wo_ref, o_ref, att_ref):
    h = h_ref[...] + _dot(mix_ref[...], wmix_ref[...])
    d = h.shape[-1]
    hd = d // XATTN_HEADS
    hn = _bf(_rms(h, nw_ref[...]))
    q = _bf(_dot(hn, wq_ref[...]) * (hd ** -0.5))
    for i in range(XATTN_HEADS):
        cs = slice(i * hd, (i + 1) * hd)
        s = _dot_nt(q[:, cs], k_ref[0, :, cs])
        s = s - jnp.max(s, axis=-1, keepdims=True)
        p = jnp.exp(s)
        p = p / jnp.sum(p, axis=-1, keepdims=True)
        att_ref[:, cs] = _bf(_dot(_bf(p), v_ref[0, :, cs]))
    o_ref[...] = h + _dot(att_ref[...], wo_ref[...])


def _xattn(mix, w_mix, h, nw, wq, kv, wo, batch, tq):
    m, d = h.shape
    t = m // batch
    mem_len = kv.shape[1]
    nt = t // tq
    row_map = lambda b, i: (b * nt + i, 0)
    const_map = lambda b, i: (0, 0)
    weight = lambda: pl.BlockSpec((d, d), const_map, pipeline_mode=pl.Buffered(1))
    return pl.pallas_call(
        _xattn_body,
        out_shape=jax.ShapeDtypeStruct((m, d), F32),
        grid=(batch, nt),
        in_specs=[
            pl.BlockSpec((tq, d), row_map),
            weight(),
            pl.BlockSpec((tq, d), row_map),
            pl.BlockSpec((1, d), const_map),
            weight(),
            pl.BlockSpec((1, mem_len, d), lambda b, i: (b, 0, 0)),
            pl.BlockSpec((1, mem_len, d), lambda b, i: (b, 0, 1)),
            weight(),
        ],
        out_specs=pl.BlockSpec((tq, d), row_map),
        scratch_shapes=[pltpu.VMEM((tq, d), BF16)],
        compiler_params=pltpu.CompilerParams(
            dimension_semantics=("parallel", "parallel"),
            vmem_limit_bytes=VMEM_LIMIT),
        name="xattn",
    )(mix, w_mix, h, nw.reshape(1, d), wq, kv, kv, wo)


def _shift_rows(x, k, tail, row):
    y = pltpu.roll(x, k, 0)
    head = y[0:8]
    for j in range(k):
        src = tail.shape[0] - k + j
        head = jnp.where(row[0:8] == j, tail[src:src + 1, :], head)
    return jnp.concatenate([head, y[8:]], axis=0)


def _conv_ffn_down_body(u_ref, halo_ref, cw_ref, wd_ref, r_ref, fw_ref, o_ref,
                        ext_ref, yst_ref, act_ref, *, tiles_per_seq, final_norm):
    tm = u_ref.shape[0]
    d_ff = act_ref.shape[1]
    first = (pl.program_id(0) % tiles_per_seq) == 0
    keep = jnp.where(first, 0.0, 1.0)
    hrows = halo_ref.shape[0]
    nq = tm // CONV_PHASES

    def phase_rows(half, off):
        return ext_ref[half, pl.ds(8 + off, nq, stride=CONV_PHASES), :]

    for c0 in range(0, d_ff, LANES):
        for half in range(2):
            cs = slice(half * d_ff + c0, half * d_ff + c0 + LANES)
            ext_ref[half, 0:8, :] = halo_ref[hrows - 8:hrows, cs].astype(F32) * keep
            ext_ref[half, 8:8 + tm, :] = u_ref[:, cs].astype(F32)
        xs = [[phase_rows(half, off) for off in range(1 - FFN_CONV, CONV_PHASES)]
              for half in range(2)]
        ws = [cw_ref[:, half * d_ff + c0:half * d_ff + c0 + LANES] for half in range(2)]
        for p in range(CONV_PHASES):
            ys = []
            for half in range(2):
                y = None
                for k in range(FFN_CONV):
                    t = xs[half][p + k] * ws[half][k:k + 1, :]
                    y = t if y is None else y + t
                ys.append(y)
            hg = ys[0]
            yst_ref[pl.ds(p, nq, stride=CONV_PHASES), :] = (hg + hg * jnp.tanh(hg)) * ys[1]
        act_ref[:, c0:c0 + LANES] = _bf(yst_ref[...])
    out = r_ref[...] + _dot(act_ref[...], wd_ref[...])
    if final_norm:
        out = _rms(out, fw_ref[...])
    o_ref[...] = out


def _conv_ffn_down(u, conv_w, w_down, res, final_w, seq_len, tm, final_norm):
    m, n2 = u.shape
    d_ff = n2 // 2
    d = w_down.shape[1]
    halo = 16
    body = functools.partial(_conv_ffn_down_body, tiles_per_seq=seq_len // tm,
                             final_norm=final_norm)
    return pl.pallas_call(
        body,
        out_shape=jax.ShapeDtypeStruct((m, d), F32),
        grid=(m // tm,),
        in_specs=[
            pl.BlockSpec((tm, n2), lambda i: (i, 0)),
            pl.BlockSpec((halo, n2), lambda i: (jnp.maximum(i * (tm // halo) - 1, 0), 0)),
            pl.BlockSpec((FFN_CONV, n2), lambda i: (0, 0)),
            pl.BlockSpec((d_ff, d), lambda i: (0, 0), pipeline_mode=pl.Buffered(1)),
            pl.BlockSpec((tm, d), lambda i: (i, 0)),
            pl.BlockSpec((1, d), lambda i: (0, 0)),
        ],
        out_specs=pl.BlockSpec((tm, d), lambda i: (i, 0)),
        scratch_shapes=[pltpu.VMEM((2, 8 + tm, LANES), F32),
                        pltpu.VMEM((tm, LANES), F32),
                        pltpu.VMEM((tm, d_ff), BF16)],
        compiler_params=pltpu.CompilerParams(
            dimension_semantics=("parallel",),
            vmem_limit_bytes=VMEM_LIMIT),
        name="conv_ffn_down",
    )(u, u, conv_w, w_down, res, final_w.reshape(1, d))


def _mixer_constants(tb):
    c = CHUNK
    ar = np.arange
    std_head = ar(GROUP) // HEAD_DIM
    rope_head = (ar(GROUP) % LANES) // ROPE_HALF
    gla_head = ar(GLA_KEYS) // GLA_KEY_DIM
    row_head = ar(GROUP) // c
    s_of = ar(GROUP) % c
    t = ar(c)
    tt = ar(tb) % c
    same_chunk = (ar(tb)[:, None] // c) == (ar(tb)[None, :] // c)

    out = {}
    out["seg"] = (std_head[:, None] == std_head[None, :], BF16)

    log_gamma = np.log(1.0 - np.exp2(-5.0 - ar(N_HEADS, dtype=np.float64)))
    scale = HEAD_DIM ** -0.5
    rel = t[:, None] - s_of[None, :]
    lg_w = log_gamma[std_head][None, :]
    out["ret_d"] = (np.where(rel >= 0, np.exp(lg_w * np.maximum(rel, 0)), 0.0) * scale, F32)
    lg_r = log_gamma[rope_head][None, :]
    out["ret_qdec"] = (np.exp(lg_r * (tt[:, None] + 1.0)), F32)
    out["ret_kdec"] = (np.exp(lg_r * (c - 1.0 - tt[:, None])) * scale, F32)
    out["ret_cdec"] = (np.exp(log_gamma[std_head] * c)[None, :], F32)

    lvl, sel_q, sel_k = [], [], []
    for b in LEVELS:
        blk = t // (2 * b)
        second = (t % (2 * b)) >= b
        bound = blk * 2 * b + b - 1
        msk = (blk[:, None] == blk[None, :]) & second[:, None] & (~second)[None, :]
        lvl.append(msk[:, s_of])
        sel_q.append(second[:, None] & (t[None, :] > bound[:, None]) & (t[None, :] <= t[:, None]))
        sel_k.append((~second)[:, None] & (t[None, :] > t[:, None]) & (t[None, :] <= bound[:, None]))
    lvl.append(t[:, None] == s_of[None, :])
    out["lvl"] = (np.stack(lvl), F32)
    tri = t[None, :] <= t[:, None]
    tail = t[None, :] > t[:, None]
    eye_b = np.eye(tb // c, dtype=bool)
    sel = [np.kron(eye_b, s_) for s_ in [a | b for a, b in zip(sel_q, sel_k)] + [tri, tail]]
    out["sel"] = (np.stack(sel), BF16)
    out["ones_b"] = (same_chunk, BF16)
    out["trit_w"] = (tt[:, None] <= s_of[None, :], F32)
    out["causal_w"] = (tt[:, None] >= s_of[None, :], F32)
    out["strict_w"] = (tt[:, None] > s_of[None, :], F32)
    out["eye_w"] = (t[:, None] == s_of[None, :], F32)
    eab = np.zeros((LANES, 2 * GROUP))
    eab[SM_BA + std_head, ar(GROUP)] = 1.0
    eab[SM_BB + std_head, GROUP + ar(GROUP)] = 1.0
    out["exp_ab"] = (eab, BF16)
    return out


_CONST_ORDER = ("seg", "ret_d", "ret_qdec", "ret_kdec",
                "ret_cdec", "lvl", "sel", "ones_b", "trit_w", "causal_w", "strict_w", "eye_w",
                "exp_ab")
_PARAM_ORDER = ("gdn_conv", "gdn_alog", "gdn_dt", "gdn_nw", "gla_up", "gla_bias", "gla_nw",
                "hg_loglb", "hg_log1mlb", "hg_1mlb", "hg_nw")
_BD_SCRATCH = ("a_k", "a_kd", "a_v", "b_k", "b_kt", "b_rhs", "b_t", "b_x", "b_u",
               "c_kx", "c_v", "d_kx", "d_v")
_SCRATCH_ORDER = ("sa", "sb", "sct", "sdt", "tail") + _BD_SCRATCH + (
    "aq", "aqd", "bq", "bk", "bqg", "bbr", "brc", "bcd", "cqx", "ccd", "dqx", "dcd", "oscr")


def _mixer_scratch(tb):
    f, b = F32, BF16
    rows = tb // CHUNK * GROUP
    nl = N_LEVELS + 2
    shapes = {k: ((rows, GROUP), b) for k in _BD_SCRATCH}
    shapes["b_rhs"] = ((rows, 2 * GROUP), b)
    shapes["c_kx"] = ((nl, rows, GLA_KEYS), b)
    shapes["d_kx"] = ((nl, rows, GROUP), b)
    shapes.update({
        "sa": ((GROUP, GROUP), f), "sb": ((GROUP, GROUP), f),
        "sct": ((GROUP, GLA_KEYS), f), "sdt": ((GROUP, GROUP), f),
        "tail": ((8, 3 * GROUP), f),
        "aq": ((tb, GROUP), b), "aqd": ((tb, GROUP), b),
        "bq": ((tb, GROUP), b), "bk": ((tb, GROUP), b), "bqg": ((tb, GROUP), b),
        "bbr": ((tb, GROUP), f), "brc": ((tb, GROUP), f), "bcd": ((tb, GROUP), f),
        "cqx": ((nl, tb, GLA_KEYS), b), "ccd": ((tb, GLA_KEYS), f),
        "dqx": ((nl, tb, GROUP), b), "dcd": ((tb, GROUP), f),
        "oscr": ((tb, N_HEADS * GROUP), f),
    })
    return [pltpu.VMEM(*shapes[k]) for k in _SCRATCH_ORDER]


def _mixer_body(*refs, n_chunks):
    zb_ref, zf_ref, cos_ref, sin_ref = refs[:4]
    nc, npar = len(_CONST_ORDER), len(_PARAM_ORDER)
    cst = dict(zip(_CONST_ORDER, refs[4:4 + nc]))
    par = dict(zip(_PARAM_ORDER, refs[4 + nc:4 + nc + npar]))
    o_ref = refs[4 + nc + npar]
    scr = dict(zip(_SCRATCH_ORDER, refs[5 + nc + npar:]))
    tb = zb_ref.shape[0]
    rows = [slice(n * CHUNK, (n + 1) * CHUNK) for n in range(n_chunks)]
    hd = HEAD_DIM

    def zcols(col, width):
        return zb_ref[:, col:col + width].astype(F32)

    @pl.when(pl.program_id(1) == 0)
    def _():
        for k in ("sa", "sb", "sct", "sdt", "tail"):
            scr[k][...] = jnp.zeros_like(scr[k])

    @pl.when((pl.program_id(0) == 0) & (pl.program_id(1) == 0))
    def _():
        for k in _BD_SCRATCH:
            scr[k][...] = jnp.zeros_like(scr[k])

    def seg_sum(x):
        return _dot(_bf(x), cst["seg"][...])

    def bd(n):
        return slice(n * GROUP, (n + 1) * GROUP)

    def put_chunk(ref, n, x, lead=(), lane0=0):
        for h in range(N_HEADS):
            ref[lead + (slice(n * GROUP + h * hd, n * GROUP + (h + 1) * hd),
                        slice(lane0 + h * hd, lane0 + (h + 1) * hd))] = x[:, h * hd:(h + 1) * hd]

    def put_std(ref, x, lead=(), lane0=0):
        for n, rs in enumerate(rows):
            put_chunk(ref, n, x[rs], lead, lane0)

    def put_rope(ref, x):
        for n, rs in enumerate(rows):
            for h in range(N_HEADS):
                for half in range(2):
                    l0 = half * LANES + h * ROPE_HALF
                    ref[n * GROUP + h * hd:n * GROUP + (h + 1) * hd, l0:l0 + ROPE_HALF] = (
                        x[rs, l0:l0 + ROPE_HALF])

    def put_gla(ref, x, lead=()):
        for n, rs in enumerate(rows):
            for h in range(N_HEADS):
                l0 = h * GLA_KEY_DIM
                ref[lead + (slice(n * GROUP + h * hd, n * GROUP + (h + 1) * hd),
                            slice(l0, l0 + GLA_KEY_DIM))] = x[rs, l0:l0 + GLA_KEY_DIM]

    cos = cos_ref[...]
    sin = sin_ref[...]

    def rope(col):
        x1 = zcols(col, LANES)
        x2 = zcols(col + LANES, LANES)
        return jnp.concatenate([x1 * cos - x2 * sin, x1 * sin + x2 * cos], axis=1)

    qr = rope(Z_RQ)
    scr["aq"][...] = _bf(qr)
    scr["aqd"][...] = _bf(qr * cst["ret_qdec"][...])
    kr = rope(Z_RK)
    put_rope(scr["a_k"], _bf(kr))
    put_rope(scr["a_kd"], _bf(kr * cst["ret_kdec"][...]))
    put_std(scr["a_v"], zb_ref[:, Z_RV:Z_RV + GROUP])

    row = lax.broadcasted_iota(jnp.int32, (tb, 1), 0)
    x = zcols(Z_BQKV, 3 * GROUP)
    tail = scr["tail"][...]
    cw = par["gdn_conv"][...]
    y = x * cw[GDN_CONV - 1:GDN_CONV, :]
    for k in range(1, GDN_CONV):
        y = y + _shift_rows(x, k, tail, row) * cw[GDN_CONV - 1 - k:GDN_CONV - k, :]
    scr["tail"][...] = x[tb - 8:tb, :]
    qkv = _silu(y)
    qb, kb, vv = qkv[:, 0:GROUP], qkv[:, GROUP:2 * GROUP], qkv[:, 2 * GROUP:3 * GROUP]
    sm = zf_ref[:, F_SMALL:F_SMALL + LANES]
    bab = _dot_sel(sm, cst["exp_ab"][...])
    g = -jnp.exp(par["gdn_alog"][...]) * _softplus(bab[:, 0:GROUP] + par["gdn_dt"][...])
    beta = jax.nn.sigmoid(bab[:, GROUP:2 * GROUP])
    qn = qb * lax.rsqrt(seg_sum(qb * qb) + EPS) * (HEAD_DIM ** -0.5)
    kn = kb * lax.rsqrt(seg_sum(kb * kb) + EPS)
    ones_b = cst["ones_b"][...]
    gc = _sel_dot(cst["sel"][N_LEVELS], g)
    gcr = _sel_dot(ones_b, g * cst["trit_w"][...])
    glast = _sel_dot(ones_b, g)
    dec = jnp.exp(gc - gcr)
    gam = jnp.exp(gc)
    knb = _bf(kn)
    scr["bq"][...] = _bf(qn)
    scr["bk"][...] = knb
    put_std(scr["b_k"], knb)
    scr["bbr"][...] = beta * jnp.where(cst["strict_w"][...] > 0, dec, 0.0)
    scr["brc"][...] = jnp.where(cst["causal_w"][...] > 0, dec, 0.0)
    put_std(scr["b_rhs"], _bf(beta * vv))
    put_std(scr["b_rhs"], _bf(beta * gam * kn), lane0=GROUP)
    scr["bqg"][...] = _bf(qn * gam)
    put_std(scr["b_kt"], _bf(kn * jnp.exp(glast - gc)))
    scr["bcd"][...] = jnp.exp(glast)

    def gl_prep(q, k, v, logf, qx_ref, kx_ref, v_ref, cd_ref, put_k):
        hi, lo = _split(logf, 2)
        for j in range(N_LEVELS + 2):
            sel = cst["sel"][j]
            e = jnp.exp(_dot(sel, hi) + _dot(sel, lo))
            if j < N_LEVELS:
                qx_ref[j] = _bf(q * e)
                put_k(kx_ref, _bf(k * e), (j,))
            elif j == N_LEVELS:
                qx_ref[GL_EDGE] = _bf(q * e)
                cd_ref[...] = e
            else:
                put_k(kx_ref, _bf(k * e), (GL_EDGE,))
            yield
        qx_ref[GL_DIAG] = _bf(q)
        put_k(kx_ref, _bf(k), (GL_DIAG,))
        put_std(v_ref, v)

    logit = _dot_hp(sm, par["gla_up"][...]) + par["gla_bias"][...]
    prep_c = gl_prep(zcols(Z_CQ, GLA_KEYS) * (GLA_KEY_DIM ** -0.5),
                     zcols(Z_CK, GLA_KEYS), zb_ref[:, Z_CV:Z_CV + GROUP],
                     _log_sigmoid(logit) * (1.0 / GLA_GATE_NORM),
                     scr["cqx"], scr["c_kx"], scr["c_v"], scr["ccd"], put_gla)

    f_pre = zf_ref[:, F_DF:F_DF + GROUP]
    a = par["hg_loglb"][...]
    b = par["hg_log1mlb"][...] + _log_sigmoid(f_pre)
    logf_d = jnp.maximum(a, b) + jnp.log1p(jnp.exp(-jnp.abs(a - b)))
    prep_d = gl_prep(zcols(Z_DQ, GROUP), par["hg_1mlb"][...] * jax.nn.sigmoid(-f_pre),
                     zb_ref[:, Z_DI:Z_DI + GROUP], logf_d,
                     scr["dqx"], scr["d_kx"], scr["d_v"], scr["dcd"], put_std)

    pa, pb, pc, pd = {}, {}, {}, {}

    def ret_p1(n):
        rs = rows[n]
        p = _dot_nt(scr["aq"][rs, :], scr["a_k"][bd(n), :])
        yield
        pa[n] = _bf(p * cst["ret_d"][...])

    def gl_p1(n, qx_ref, kx_ref, store):
        rs = rows[n]
        p = None
        for l in range(N_LEVELS + 1):
            t = _dot_nt(qx_ref[l, rs, :], kx_ref[l, bd(n), :])
            yield
            t = t * cst["lvl"][l]
            p = t if p is None else p + t
        store[n] = _bf(p)

    def wide_matmul(a, b, ref, n):
        put_chunk(ref, n, _bf(b))
        return _dot(_bf(a), ref[bd(n), :])

    def gdn_p1(n):
        rs = rows[n]
        kbd = scr["b_k"][bd(n), :]
        kk = _dot_nt(scr["bk"][rs, :], kbd)
        qk = _dot_nt(scr["bq"][rs, :], kbd)
        yield
        lmat = scr["bbr"][rs, :] * kk
        tinv = cst["eye_w"][...] - lmat * cst["lvl"][0]
        for li in range(1, N_LEVELS):
            xb = wide_matmul(lmat * cst["lvl"][li], tinv, scr["b_t"], n)
            yield
            tinv = tinv - wide_matmul(tinv, xb, scr["b_x"], n)
            yield
        uw = _dot(_bf(tinv), scr["b_rhs"][bd(n), :])
        yield
        pb[n] = (uw[:, 0:GROUP], _bf(uw[:, GROUP:2 * GROUP]), _bf(qk * scr["brc"][rs, :]))


    oscr = scr["oscr"]

    def ret_p2():
        s = scr["sa"][...]
        for n, rs in enumerate(rows):
            vbd = scr["a_v"][bd(n), :]
            o = _dot(pa[n], vbd) + _dot(scr["aqd"][rs, :], _bf(s))
            kv = _dot_tn(scr["a_kd"][bd(n), :], vbd)
            yield
            s = s * cst["ret_cdec"][...] + kv
            oscr[rs, 0:GROUP] = o
        scr["sa"][...] = s

    def gdn_p2():
        s = scr["sb"][...]
        for n, rs in enumerate(rows):
            u_v, w_k, qk = pb[n]
            r = _dot(jnp.concatenate([w_k, scr["bqg"][rs, :]], axis=0), _bf(s))
            yield
            u = u_v - r[0:CHUNK]
            put_chunk(scr["b_u"], n, _bf(u))
            ubd = scr["b_u"][bd(n), :]
            o = r[CHUNK:2 * CHUNK] + _dot(qk, ubd)
            kv = _dot_tn(scr["b_kt"][bd(n), :], ubd)
            yield
            s = s * scr["bcd"][rs.start:rs.start + 1, :] + kv
            oscr[rs, GROUP:2 * GROUP] = o
        scr["sb"][...] = s

    def gl_p2(store, qx_ref, kx_ref, v_ref, cd_ref, st_ref, col):
        st = st_ref[...]
        for n, rs in enumerate(rows):
            vbd = v_ref[bd(n), :]
            o = _dot(store[n], vbd) + _dot_nt(qx_ref[GL_EDGE, rs, :], _bf(st))
            kv = _dot_tn(vbd, kx_ref[GL_EDGE, bd(n), :])
            yield
            st = st * cd_ref[rs.stop - 1:rs.stop, :] + kv
            oscr[rs, col:col + GROUP] = o
        st_ref[...] = st

    def finish(idx):
        inv_hd = 1.0 / HEAD_DIM
        gains = (None, par["gdn_nw"], par["gla_nw"], par["hg_nw"])
        gates = (Z_RG, Z_BG, Z_CG, Z_DG)
        for i in idx:
            o = oscr[:, i * GROUP:(i + 1) * GROUP]
            o = o * lax.rsqrt(seg_sum(o * o) * inv_hd + EPS)
            yield
            if gains[i] is not None:
                o = o * gains[i][...]
            o_ref[:, i * GROUP:(i + 1) * GROUP] = _bf(o * _silu(zcols(gates[i], GROUP)))

    _run_interleaved([g for n in range(n_chunks) for g in (gdn_p1(n), ret_p1(n))]
                     + [prep_c, prep_d])
    _run_interleaved([g for n in range(n_chunks)
                      for g in (gl_p1(n, scr["cqx"], scr["c_kx"], pc),
                                gl_p1(n, scr["dqx"], scr["d_kx"], pd))]
                     + [gdn_p2(), ret_p2()])
    _run_interleaved([
        gl_p2(pc, scr["cqx"], scr["c_kx"], scr["c_v"], scr["ccd"], scr["sct"], 2 * GROUP),
        gl_p2(pd, scr["dqx"], scr["d_kx"], scr["d_v"], scr["dcd"], scr["sdt"], 3 * GROUP),
        finish((0, 1))])
    _run_interleaved([finish((2, 3))])


def _mixer(zb, zf, cos_t, sin_t, params, batch, tb):
    m = zb.shape[0]
    t = m // batch
    nt = t // tb
    consts = {k: jnp.asarray(v, dt) for k, (v, dt) in _mixer_constants(tb).items()}
    const_args = [consts[k] for k in _CONST_ORDER]
    param_args = [params[k] for k in _PARAM_ORDER]

    def full_spec(a):
        nd = a.ndim
        return pl.BlockSpec(a.shape, lambda b, i, _nd=nd: (0,) * _nd)

    row_map = lambda b, i: (b * nt + i, 0)
    body = functools.partial(_mixer_body, n_chunks=tb // CHUNK)
    return pl.pallas_call(
        body,
        out_shape=jax.ShapeDtypeStruct((m, N_HEADS * GROUP), BF16),
        grid=(batch, nt),
        in_specs=[pl.BlockSpec((tb, Z_WIDTH), row_map),
                  pl.BlockSpec((tb, F_WIDTH), row_map),
                  pl.BlockSpec((tb, LANES), row_map),
                  pl.BlockSpec((tb, LANES), row_map)]
                 + [full_spec(a) for a in const_args]
                 + [full_spec(a) for a in param_args],
        out_specs=pl.BlockSpec((tb, N_HEADS * GROUP), row_map),
        scratch_shapes=_mixer_scratch(tb),
        compiler_params=pltpu.CompilerParams(
            dimension_semantics=("arbitrary", "arbitrary"),
            vmem_limit_bytes=VMEM_LIMIT),
        name="mixer",
    )(zb, zf, cos_t, sin_t, *const_args, *param_args)


def _in_proj_segments():
    src = {}
    off = 0
    for name, width in (("rq", 256), ("rk", 256), ("rv", 256), ("rg", 256),
                        ("bq", 256), ("bk", 256), ("bv", 256), ("ba", 4), ("bb", 4), ("bg", 256),
                        ("cq", 128), ("ck", 128), ("cv", 256), ("clr", 16), ("cg", 256),
                        ("dq", 256), ("df", 256), ("di", 256), ("dg", 256)):
        src[name] = off
        off += width

    def rope(name):
        return [(src[name] + h * HEAD_DIM + half * ROPE_HALF, ROPE_HALF)
                for half in range(2) for h in range(N_HEADS)]

    segs = rope("rq") + rope("rk")
    segs += [(src["rv"], 256), (src["rg"], 256), (src["bq"], 768), (src["bg"], 256),
             (src["cq"], 128), (src["ck"], 128), (src["cv"], 256), (src["cg"], 256),
             (src["dq"], 256), (src["di"], 256), (src["dg"], 256)]
    assert sum(w for _, w in segs) == Z_WIDTH
    segs += [(src["df"], 256), (src["ba"], 8), (src["clr"], GLA_LOWRANK),
             (None, LANES - SM_LR - GLA_LOWRANK)]
    assert sum(w for _, w in segs) == Z_WIDTH + F_WIDTH
    return segs


def kernel(x, mem, positions, mix_norm_w, w_in, gdn_conv_w, gdn_a_log, gdn_dt_bias, gdn_norm_w,
           gla_gk_up, gla_gk_bias, gla_norm_w, hgrn_lb_logits, hgrn_norm_w, w_out,
           xattn_norm_w, mem_norm_w, xattn_wq, xattn_wk, xattn_wv, xattn_wo,
           ffn_norm_w, ffn_up, ffn_conv_w, ffn_down, final_norm_w):
    batch, seq, d = x.shape
    depth = w_in.shape[0]
    m = batch * seq
    mem_len = mem.shape[1]

    w_in_p = jnp.concatenate(
        [jnp.zeros((depth, d, wd), BF16) if s0 is None else w_in[:, :, s0:s0 + wd].astype(BF16)
         for s0, wd in _in_proj_segments()], axis=2)

    freqs = ROPE_BASE ** (-jnp.arange(0, HEAD_DIM, 2, dtype=F32) / HEAD_DIM)
    freq_row = jnp.tile(freqs, N_HEADS).reshape(1, LANES)
    cos_t, sin_t = _rope_tables(positions.reshape(m, 1), freq_row, 1024)

    lb_all = jnp.cumsum(jax.nn.softmax(hgrn_lb_logits.astype(F32), axis=0), axis=0)
    lb_all = lb_all - lb_all[0]

    def head_row(v):
        return jnp.tile(v.astype(F32), N_HEADS).reshape(1, GROUP)

    h = x.reshape(m, d)
    mem2 = mem.reshape(batch * mem_len, d)
    for l in range(depth):
        gla_up = jnp.zeros((LANES, GLA_KEYS), F32).at[SM_LR:SM_LR + GLA_LOWRANK].set(gla_gk_up[l])
        params = {
            "gdn_conv": gdn_conv_w[l].astype(F32),
            "gdn_alog": jnp.repeat(gdn_a_log[l].astype(F32), HEAD_DIM).reshape(1, GROUP),
            "gdn_dt": jnp.repeat(gdn_dt_bias[l].astype(F32), HEAD_DIM).reshape(1, GROUP),
            "gdn_nw": head_row(gdn_norm_w[l]),
            "gla_up": gla_up,
            "gla_bias": gla_gk_bias[l].astype(F32).reshape(1, GLA_KEYS),
            "gla_nw": head_row(gla_norm_w[l]),
            "hg_loglb": jnp.log(lb_all[l]).reshape(1, GROUP),
            "hg_log1mlb": jnp.log1p(-lb_all[l]).reshape(1, GROUP),
            "hg_1mlb": (1.0 - lb_all[l]).reshape(1, GROUP),
            "hg_nw": head_row(hgrn_norm_w[l]),
        }
        zb, zf = _norm_matmul(h, mix_norm_w[l], w_in_p[l],
                              ((Z_WIDTH, BF16), (F_WIDTH, F32)), 1024)
        o = _mixer(zb, zf, cos_t, sin_t, params, batch, 256)

        w_kv = jnp.concatenate([xattn_wk[l], xattn_wv[l]], axis=1).astype(BF16)
        kv, = _norm_matmul(mem2, mem_norm_w, w_kv, ((2 * d, BF16),), 512)
        h = _xattn(o, w_out[l].astype(BF16), h, xattn_norm_w[l], xattn_wq[l].astype(BF16),
                   kv.reshape(batch, mem_len, 2 * d), xattn_wo[l].astype(BF16), batch, 512)

        u, = _norm_matmul(h, ffn_norm_w[l], ffn_up[l].astype(BF16), ((ffn_up.shape[2], BF16),), 512)
        d_ff = ffn_down.shape[1]
        conv_w = ffn_conv_w[l].astype(F32) * jnp.where(jnp.arange(2 * d_ff) < d_ff, 0.5, 1.0)
        h = _conv_ffn_down(u, conv_w, ffn_down[l].astype(BF16), h,
                           final_norm_w, seq, 512, l == depth - 1)
    return h.reshape(batch, seq, d)
```

```python
import functools

import numpy as np
import jax
import jax.numpy as jnp
from jax import lax
from jax.experimental import pallas as pl
from jax.experimental.pallas import tpu as pltpu

F32 = jnp.float32
BF16 = jnp.bfloat16

N_HEADS = 4
HEAD_DIM = 64
GROUP = N_HEADS * HEAD_DIM
GLA_KEY_DIM = 32
GLA_KEYS = N_HEADS * GLA_KEY_DIM
GLA_LOWRANK = 16
GLA_GATE_NORM = 16.0
GDN_CONV = 4
CHUNK = 64
ROPE_BASE = 10000.0
ROPE_HALF = HEAD_DIM // 2
XATTN_HEADS = 4
FFN_CONV = 3
CONV_PHASES = 4
EPS = 1e-6

LANES = 128
VMEM_LIMIT = 52 * 1024 * 1024

TM_IN_PROJ = 1024
TM_KV = 512
TQ_XATTN = 512
TM_FFN_UP = 512
TM_FFN_DOWN = 512
TB_MIXER = 256

Z_RQ, Z_RK, Z_RV, Z_RG = 0, 256, 512, 768
Z_BQKV, Z_BG = 1024, 1792
Z_CQ, Z_CK, Z_CV, Z_CG = 2048, 2176, 2304, 2560
Z_DQ, Z_DI, Z_DG = 2816, 3072, 3328
Z_WIDTH = 3584
F_DF, F_SMALL = 0, 256
F_WIDTH = 384
SM_BA, SM_BB, SM_LR = 0, 4, 8

LEVELS = (1, 2, 4, 8, 16, 32)
N_LEVELS = len(LEVELS)
GL_DIAG = N_LEVELS
GL_EDGE = N_LEVELS + 1


def _bf(x):
    return x.astype(BF16)


def _dot(a, b):
    return jnp.dot(a, b, preferred_element_type=F32)


def _dot_nt(a, b):
    return lax.dot_general(a, b, (((1,), (1,)), ((), ())), preferred_element_type=F32)


def _dot_tn(a, b):
    return lax.dot_general(a, b, (((0,), (0,)), ((), ())), preferred_element_type=F32)


def _split(x, n):
    parts = []
    r = x
    for i in range(n):
        p = r.astype(BF16)
        parts.append(p)
        if i < n - 1:
            r = r - p.astype(F32)
    return parts


def _sel_dot(c, x, n=2):
    acc = None
    for p in _split(x, n):
        t = _dot(c, p)
        acc = t if acc is None else acc + t
    return acc


def _dot_sel(x, c, n=2):
    acc = None
    for p in _split(x, n):
        t = _dot(p, c)
        acc = t if acc is None else acc + t
    return acc


def _dot_hp(a, b):
    a_hi, a_lo = _split(a, 2)
    b_hi, b_lo = _split(b, 2)
    return _dot(a_hi, b_hi) + _dot(a_lo, b_hi) + _dot(a_hi, b_lo)


def _silu(x):
    return x * jax.nn.sigmoid(x)


def _log_sigmoid(x):
    return jnp.minimum(x, 0.0) - jnp.log1p(jnp.exp(-jnp.abs(x)))


def _softplus(x):
    return jnp.maximum(x, 0.0) + jnp.log1p(jnp.exp(-jnp.abs(x)))


def _rms(x, w):
    ms = jnp.mean(x * x, axis=-1, keepdims=True)
    return x * lax.rsqrt(ms + EPS) * w


def _blockdiag(x, mask):
    return jnp.concatenate([x, x, x, x], axis=0) * mask


def _run_interleaved(gens):
    gens = list(gens)
    while gens:
        alive = []
        for g in gens:
            try:
                next(g)
                alive.append(g)
            except StopIteration:
                pass
        gens = alive


def _col_chunks(n, width):
    return [(c0, min(width, n - c0)) for c0 in range(0, n, width)]


def _norm_matmul_body(x_ref, nw_ref, w_ref, *o_refs, widths, col_chunk):
    xn = _bf(_rms(x_ref[...], nw_ref[...]))
    base = 0
    for o_ref, width in zip(o_refs, widths):
        for c0, cw in _col_chunks(width, col_chunk):
            o_ref[:, c0:c0 + cw] = _dot(xn, w_ref[:, base + c0:base + c0 + cw]).astype(o_ref.dtype)
        base += width


def _norm_matmul(x, nw, w, outs, tm):
    m, d = x.shape
    n = w.shape[1]
    widths = tuple(wd for wd, _ in outs)
    assert sum(widths) == n
    body = functools.partial(_norm_matmul_body, widths=widths, col_chunk=1024)
    res = pl.pallas_call(
        body,
        out_shape=tuple(jax.ShapeDtypeStruct((m, wd), dt) for wd, dt in outs),
        grid=(m // tm,),
        in_specs=[
            pl.BlockSpec((tm, d), lambda i: (i, 0)),
            pl.BlockSpec((1, d), lambda i: (0, 0)),
            pl.BlockSpec((d, n), lambda i: (0, 0), pipeline_mode=pl.Buffered(1)),
        ],
        out_specs=tuple(pl.BlockSpec((tm, wd), lambda i: (i, 0)) for wd, _ in outs),
        compiler_params=pltpu.CompilerParams(
            dimension_semantics=("parallel",),
            vmem_limit_bytes=VMEM_LIMIT),
        name="norm_matmul",
    )(x, nw.reshape(1, d), w)
    return res


def _rope_table_body(pos_ref, freq_ref, cos_ref, sin_ref):
    ang = pos_ref[...].astype(F32) * freq_ref[...]
    cos_ref[...] = jnp.cos(ang)
    sin_ref[...] = jnp.sin(ang)


def _rope_tables(pos, freq_row, tm):
    m = pos.shape[0]
    return pl.pallas_call(
        _rope_table_body,
        out_shape=(jax.ShapeDtypeStruct((m, LANES), F32),
                   jax.ShapeDtypeStruct((m, LANES), F32)),
        grid=(m // tm,),
        in_specs=[
            pl.BlockSpec((tm, 1), lambda i: (i, 0)),
            pl.BlockSpec((1, LANES), lambda i: (0, 0)),
        ],
        out_specs=(pl.BlockSpec((tm, LANES), lambda i: (i, 0)),
                   pl.BlockSpec((tm, LANES), lambda i: (i, 0))),
        compiler_params=pltpu.CompilerParams(dimension_semantics=("parallel",)),
        name="rope_tables",
    )(pos, freq_row)


def _xattn_body(mix_ref, wmix_ref, h_ref, nw_ref, wq_ref, k_ref, v_ref, wo_ref, o_ref, att_ref):
    h = h_ref[...] + _dot(mix_ref[...], wmix_ref[...])
    d = h.shape[-1]
    hd = d // XATTN_HEADS
    hn = _bf(_rms(h, nw_ref[...]))
    q = _bf(_dot(hn, wq_ref[...]) * (hd ** -0.5))
    for i in range(XATTN_HEADS):
        cs = slice(i * hd, (i + 1) * hd)
        s = _dot_nt(q[:, cs], k_ref[0, :, cs])
        s = s - jnp.max(s, axis=-1, keepdims=True)
        p = jnp.exp(s)
        p = p / jnp.sum(p, axis=-1, keepdims=True)
        att_ref[:, cs] = _bf(_dot(_bf(p), v_ref[0, :, cs]))
    o_ref[...] = h + _dot(att_ref[...], wo_ref[...])


def _xattn(mix, w_mix, h, nw, wq, kv, wo, batch, tq):
    m, d = h.shape
    t = m // batch
    mem_len = kv.shape[1]
    nt = t // tq
    row_map = lambda b, i: (b * nt + i, 0)
    const_map = lambda b, i: (0, 0)
    weight = lambda: pl.BlockSpec((d, d), const_map, pipeline_mode=pl.Buffered(1))
    return pl.pallas_call(
        _xattn_body,
        out_shape=jax.ShapeDtypeStruct((m, d), F32),
        grid=(batch, nt),
        in_specs=[
            pl.BlockSpec((tq, d), row_map),
            weight(),
            pl.BlockSpec((tq, d), row_map),
            pl.BlockSpec((1, d), const_map),
            weight(),
            pl.BlockSpec((1, mem_len, d), lambda b, i: (b, 0, 0)),
            pl.BlockSpec((1, mem_len, d), lambda b, i: (b, 0, 1)),
            weight(),
        ],
        out_specs=pl.BlockSpec((tq, d), row_map),
        scratch_shapes=[pltpu.VMEM((tq, d), BF16)],
        compiler_params=pltpu.CompilerParams(
            dimension_semantics=("parallel", "parallel"),
            vmem_limit_bytes=VMEM_LIMIT),
        name="xattn",
    )(mix, w_mix, h, nw.reshape(1, d), wq, kv, kv, wo)


def _shift_rows(x, k, tail, row):
    y = pltpu.roll(x, k, 0)
    head = y[0:8]
    for j in range(k):
        src = tail.shape[0] - k + j
        head = jnp.where(row[0:8] == j, tail[src:src + 1, :], head)
    return jnp.concatenate([head, y[8:]], axis=0)


def _conv_ffn_down_body(u_ref, halo_ref, cw_ref, wd_ref, r_ref, fw_ref, o_ref,
                        ext_ref, yst_ref, act_ref, *, tiles_per_seq, final_norm):
    tm = u_ref.shape[0]
    d_ff = act_ref.shape[1]
    first = (pl.program_id(0) % tiles_per_seq) == 0
    keep = jnp.where(first, 0.0, 1.0)
    hrows = halo_ref.shape[0]
    nq = tm // CONV_PHASES

    def phase_rows(half, off):
        return ext_ref[half, pl.ds(8 + off, nq, stride=CONV_PHASES), :]

    for c0 in range(0, d_ff, LANES):
        for half in range(2):
            cs = slice(half * d_ff + c0, half * d_ff + c0 + LANES)
            ext_ref[half, 0:8, :] = halo_ref[hrows - 8:hrows, cs].astype(F32) * keep
            ext_ref[half, 8:8 + tm, :] = u_ref[:, cs].astype(F32)
        xs = [[phase_rows(half, off) for off in range(1 - FFN_CONV, CONV_PHASES)]
              for half in range(2)]
        ws = [cw_ref[:, half * d_ff + c0:half * d_ff + c0 + LANES] for half in range(2)]
        for p in range(CONV_PHASES):
            ys = []
            for half in range(2):
                y = None
                for k in range(FFN_CONV):
                    t = xs[half][p + k] * ws[half][k:k + 1, :]
                    y = t if y is None else y + t
                ys.append(y)
            hg = ys[0]
            yst_ref[pl.ds(p, nq, stride=CONV_PHASES), :] = (hg + hg * jnp.tanh(hg)) * ys[1]
        act_ref[:, c0:c0 + LANES] = _bf(yst_ref[...])
    out = r_ref[...] + _dot(act_ref[...], wd_ref[...])
    if final_norm:
        out = _rms(out, fw_ref[...])
    o_ref[...] = out


def _conv_ffn_down(u, conv_w, w_down, res, final_w, seq_len, tm, final_norm):
    m, n2 = u.shape
    d_ff = n2 // 2
    d = w_down.shape[1]
    halo = 16
    body = functools.partial(_conv_ffn_down_body, tiles_per_seq=seq_len // tm,
                             final_norm=final_norm)
    return pl.pallas_call(
        body,
        out_shape=jax.ShapeDtypeStruct((m, d), F32),
        grid=(m // tm,),
        in_specs=[
            pl.BlockSpec((tm, n2), lambda i: (i, 0)),
            pl.BlockSpec((halo, n2), lambda i: (jnp.maximum(i * (tm // halo) - 1, 0), 0)),
            pl.BlockSpec((FFN_CONV, n2), lambda i: (0, 0)),
            pl.BlockSpec((d_ff, d), lambda i: (0, 0), pipeline_mode=pl.Buffered(1)),
            pl.BlockSpec((tm, d), lambda i: (i, 0)),
            pl.BlockSpec((1, d), lambda i: (0, 0)),
        ],
        out_specs=pl.BlockSpec((tm, d), lambda i: (i, 0)),
        scratch_shapes=[pltpu.VMEM((2, 8 + tm, LANES), F32),
                        pltpu.VMEM((tm, LANES), F32),
                        pltpu.VMEM((tm, d_ff), BF16)],
        compiler_params=pltpu.CompilerParams(
            dimension_semantics=("parallel",),
            vmem_limit_bytes=VMEM_LIMIT),
        name="conv_ffn_down",
    )(u, u, conv_w, w_down, res, final_w.reshape(1, d))


def _mixer_constants(tb):
    c = CHUNK
    ar = np.arange
    std_head = ar(GROUP) // HEAD_DIM
    rope_head = (ar(GROUP) % LANES) // ROPE_HALF
    gla_head = ar(GLA_KEYS) // GLA_KEY_DIM
    row_head = ar(GROUP) // c
    s_of = ar(GROUP) % c
    t = ar(c)
    tt = ar(tb) % c
    same_chunk = (ar(tb)[:, None] // c) == (ar(tb)[None, :] // c)

    out = {}
    bd_std = row_head[:, None] == std_head[None, :]
    out["bd_std"] = (bd_std, BF16)
    out["bd_std2"] = (np.concatenate([bd_std, bd_std], axis=1), BF16)
    out["bd_rope"] = (row_head[:, None] == rope_head[None, :], BF16)
    out["bd_gla"] = (row_head[:, None] == gla_head[None, :], BF16)

    log_gamma = np.log(1.0 - np.exp2(-5.0 - ar(N_HEADS, dtype=np.float64)))
    scale = HEAD_DIM ** -0.5
    rel = t[:, None] - s_of[None, :]
    lg_w = log_gamma[std_head][None, :]
    out["ret_d"] = (np.where(rel >= 0, np.exp(lg_w * np.maximum(rel, 0)), 0.0) * scale, F32)
    lg_r = log_gamma[rope_head][None, :]
    out["ret_qdec"] = (np.exp(lg_r * (tt[:, None] + 1.0)), F32)
    out["ret_kdec"] = (np.exp(lg_r * (c - 1.0 - tt[:, None])) * scale, F32)
    out["ret_cdec"] = (np.exp(log_gamma[std_head] * c)[None, :], F32)

    lvl, sel_q, sel_k = [], [], []
    for b in LEVELS:
        blk = t // (2 * b)
        second = (t % (2 * b)) >= b
        bound = blk * 2 * b + b - 1
        msk = (blk[:, None] == blk[None, :]) & second[:, None] & (~second)[None, :]
        lvl.append(msk[:, s_of])
        sel_q.append(second[:, None] & (t[None, :] > bound[:, None]) & (t[None, :] <= t[:, None]))
        sel_k.append((~second)[:, None] & (t[None, :] > t[:, None]) & (t[None, :] <= bound[:, None]))
    lvl.append(t[:, None] == s_of[None, :])
    out["lvl"] = (np.stack(lvl), F32)
    tri = t[None, :] <= t[:, None]
    tail = t[None, :] > t[:, None]
    eye_b = np.eye(tb // c, dtype=bool)
    sel = [np.kron(eye_b, s_) for s_ in [a | b for a, b in zip(sel_q, sel_k)] + [tri, tail]]
    out["sel"] = (np.stack(sel), BF16)
    out["ones_b"] = (same_chunk, BF16)
    out["trit_w"] = (tt[:, None] <= s_of[None, :], F32)
    out["causal_w"] = (tt[:, None] >= s_of[None, :], F32)
    out["strict_w"] = (tt[:, None] > s_of[None, :], F32)
    out["eye_w"] = (t[:, None] == s_of[None, :], F32)
    eab = np.zeros((LANES, 2 * GROUP))
    eab[SM_BA + std_head, ar(GROUP)] = 1.0
    eab[SM_BB + std_head, GROUP + ar(GROUP)] = 1.0
    out["exp_ab"] = (eab, BF16)
    return out


_CONST_ORDER = ("bd_std", "bd_std2", "bd_rope", "bd_gla", "ret_d", "ret_qdec", "ret_kdec",
                "ret_cdec", "lvl", "sel", "ones_b", "trit_w", "causal_w", "strict_w", "eye_w",
                "exp_ab")
_PARAM_ORDER = ("gdn_conv", "gdn_alog", "gdn_dt", "gdn_nw", "gla_up", "gla_bias", "gla_nw",
                "hg_loglb", "hg_log1mlb", "hg_1mlb", "hg_nw")
_SCRATCH_ORDER = ("sa", "sb", "sct", "sdt", "tail",
                  "aq", "aqd", "ak", "akd", "av",
                  "bq", "bk", "bbr", "brc", "brhs", "bqg", "bkt", "bcd",
                  "cqx", "ckx", "cv", "ccd", "dqx", "dkx", "dv", "dcd", "oscr")


def _mixer_scratch(tb):
    f, b = F32, BF16
    shapes = {
        "sa": ((GROUP, GROUP), f), "sb": ((GROUP, GROUP), f),
        "sct": ((GROUP, GLA_KEYS), f), "sdt": ((GROUP, GROUP), f),
        "tail": ((8, 3 * GROUP), f),
        "aq": ((tb, GROUP), b), "aqd": ((tb, GROUP), b), "ak": ((tb, GROUP), b),
        "akd": ((tb, GROUP), b), "av": ((tb, GROUP), b),
        "bq": ((tb, GROUP), b), "bk": ((tb, GROUP), b), "bbr": ((tb, GROUP), f),
        "brc": ((tb, GROUP), f), "brhs": ((tb, 2 * GROUP), b), "bqg": ((tb, GROUP), b),
        "bkt": ((tb, GROUP), b), "bcd": ((tb, GROUP), f),
        "cqx": ((N_LEVELS + 2, tb, GLA_KEYS), b), "ckx": ((N_LEVELS + 2, tb, GLA_KEYS), b),
        "cv": ((tb, GROUP), b), "ccd": ((tb, GLA_KEYS), f),
        "dqx": ((N_LEVELS + 2, tb, GROUP), b), "dkx": ((N_LEVELS + 2, tb, GROUP), b),
        "dv": ((tb, GROUP), b), "dcd": ((tb, GROUP), f),
        "oscr": ((tb, N_HEADS * GROUP), f),
    }
    return [pltpu.VMEM(*shapes[k]) for k in _SCRATCH_ORDER]


def _mixer_body(*refs, n_chunks):
    zb_ref, zf_ref, cos_ref, sin_ref = refs[:4]
    nc, npar = len(_CONST_ORDER), len(_PARAM_ORDER)
    cst = dict(zip(_CONST_ORDER, refs[4:4 + nc]))
    par = dict(zip(_PARAM_ORDER, refs[4 + nc:4 + nc + npar]))
    o_ref = refs[4 + nc + npar]
    scr = dict(zip(_SCRATCH_ORDER, refs[5 + nc + npar:]))
    tb = zb_ref.shape[0]
    rows = [slice(n * CHUNK, (n + 1) * CHUNK) for n in range(n_chunks)]
    bd_std = cst["bd_std"]

    def zcols(col, width):
        return zb_ref[:, col:col + width].astype(F32)

    @pl.when(pl.program_id(1) == 0)
    def _():
        for k in ("sa", "sb", "sct", "sdt", "tail"):
            scr[k][...] = jnp.zeros_like(scr[k])

    def seg_sum(x):
        return _dot_sel(x, bd_std[...])

    cos = cos_ref[...]
    sin = sin_ref[...]

    def rope(col):
        x1 = zcols(col, LANES)
        x2 = zcols(col + LANES, LANES)
        return jnp.concatenate([x1 * cos - x2 * sin, x1 * sin + x2 * cos], axis=1)

    qr = rope(Z_RQ)
    scr["aq"][...] = _bf(qr)
    scr["aqd"][...] = _bf(qr * cst["ret_qdec"][...])
    kr = rope(Z_RK)
    scr["ak"][...] = _bf(kr)
    scr["akd"][...] = _bf(kr * cst["ret_kdec"][...])
    scr["av"][...] = zb_ref[:, Z_RV:Z_RV + GROUP]

    row = lax.broadcasted_iota(jnp.int32, (tb, 1), 0)
    x = zcols(Z_BQKV, 3 * GROUP)
    tail = scr["tail"][...]
    cw = par["gdn_conv"][...]
    y = x * cw[GDN_CONV - 1:GDN_CONV, :]
    for k in range(1, GDN_CONV):
        y = y + _shift_rows(x, k, tail, row) * cw[GDN_CONV - 1 - k:GDN_CONV - k, :]
    scr["tail"][...] = x[tb - 8:tb, :]
    qkv = _silu(y)
    qb, kb, vv = qkv[:, 0:GROUP], qkv[:, GROUP:2 * GROUP], qkv[:, 2 * GROUP:3 * GROUP]
    sm = zf_ref[:, F_SMALL:F_SMALL + LANES]
    bab = _dot_sel(sm, cst["exp_ab"][...])
    g = -jnp.exp(par["gdn_alog"][...]) * _softplus(bab[:, 0:GROUP] + par["gdn_dt"][...])
    beta = jax.nn.sigmoid(bab[:, GROUP:2 * GROUP])
    qn = qb * lax.rsqrt(seg_sum(qb * qb) + EPS) * (HEAD_DIM ** -0.5)
    kn = kb * lax.rsqrt(seg_sum(kb * kb) + EPS)
    ones_b = cst["ones_b"][...]
    gc = _sel_dot(cst["sel"][N_LEVELS], g)
    gcr = _sel_dot(ones_b, g * cst["trit_w"][...])
    glast = _sel_dot(ones_b, g)
    dec = jnp.exp(gc - gcr)
    gam = jnp.exp(gc)
    scr["bq"][...] = _bf(qn)
    scr["bk"][...] = _bf(kn)
    scr["bbr"][...] = beta * jnp.where(cst["strict_w"][...] > 0, dec, 0.0)
    scr["brc"][...] = jnp.where(cst["causal_w"][...] > 0, dec, 0.0)
    scr["brhs"][...] = _bf(jnp.concatenate([beta * vv, beta * gam * kn], axis=1))
    scr["bqg"][...] = _bf(qn * gam)
    scr["bkt"][...] = _bf(kn * jnp.exp(glast - gc))
    scr["bcd"][...] = jnp.exp(glast)

    def gl_prep(q, k, v, logf, qx_ref, kx_ref, v_ref, cd_ref):
        hi, lo = _split(logf, 2)
        for j in range(N_LEVELS + 2):
            sel = cst["sel"][j]
            e = jnp.exp(_dot(sel, hi) + _dot(sel, lo))
            if j < N_LEVELS:
                qx_ref[j] = _bf(q * e)
                kx_ref[j] = _bf(k * e)
            elif j == N_LEVELS:
                qx_ref[GL_EDGE] = _bf(q * e)
                cd_ref[...] = e
            else:
                kx_ref[GL_EDGE] = _bf(k * e)
            yield
        qx_ref[GL_DIAG] = _bf(q)
        kx_ref[GL_DIAG] = _bf(k)
        v_ref[...] = _bf(v)

    logit = _dot_hp(sm, par["gla_up"][...]) + par["gla_bias"][...]
    prep_c = gl_prep(zcols(Z_CQ, GLA_KEYS) * (GLA_KEY_DIM ** -0.5),
                     zcols(Z_CK, GLA_KEYS), zb_ref[:, Z_CV:Z_CV + GROUP],
                     _log_sigmoid(logit) * (1.0 / GLA_GATE_NORM),
                     scr["cqx"], scr["ckx"], scr["cv"], scr["ccd"])

    f_pre = zf_ref[:, F_DF:F_DF + GROUP]
    a = par["hg_loglb"][...]
    b = par["hg_log1mlb"][...] + _log_sigmoid(f_pre)
    logf_d = jnp.maximum(a, b) + jnp.log1p(jnp.exp(-jnp.abs(a - b)))
    prep_d = gl_prep(zcols(Z_DQ, GROUP), par["hg_1mlb"][...] * jax.nn.sigmoid(-f_pre),
                     zb_ref[:, Z_DI:Z_DI + GROUP], logf_d,
                     scr["dqx"], scr["dkx"], scr["dv"], scr["dcd"])

    pa, pb, pc, pd = {}, {}, {}, {}

    def ret_p1(n):
        rs = rows[n]
        p = _dot_nt(scr["aq"][rs, :], _blockdiag(scr["ak"][rs, :], cst["bd_rope"][...]))
        yield
        pa[n] = _bf(p * cst["ret_d"][...])

    def gl_p1(n, qx_ref, kx_ref, bd_k, store):
        rs = rows[n]
        p = None
        for l in range(N_LEVELS + 1):
            t = _dot_nt(qx_ref[l, rs, :], _blockdiag(kx_ref[l, rs, :], bd_k[...]))
            yield
            t = t * cst["lvl"][l]
            p = t if p is None else p + t
        store[n] = _bf(p)

    def wide_matmul(a, b):
        return _dot(_bf(a), _blockdiag(_bf(b), bd_std[...]))

    def gdn_p1(n):
        rs = rows[n]
        knb = scr["bk"][rs, :]
        kbd = _blockdiag(knb, bd_std[...])
        kk = _dot_nt(knb, kbd)
        qk = _dot_nt(scr["bq"][rs, :], kbd)
        yield
        lmat = scr["bbr"][rs, :] * kk
        tinv = cst["eye_w"][...] - lmat * cst["lvl"][0]
        for li in range(1, N_LEVELS):
            xb = wide_matmul(lmat * cst["lvl"][li], tinv)
            yield
            tinv = tinv - wide_matmul(tinv, xb)
            yield
        uw = _dot(_bf(tinv), _blockdiag(scr["brhs"][rs, :], cst["bd_std2"][...]))
        yield
        pb[n] = (uw[:, 0:GROUP], _bf(uw[:, GROUP:2 * GROUP]), _bf(qk * scr["brc"][rs, :]))

    oscr = scr["oscr"]

    def ret_p2():
        s = scr["sa"][...]
        for n, rs in enumerate(rows):
            vbd = _blockdiag(scr["av"][rs, :], bd_std[...])
            o = _dot(pa[n], vbd) + _dot(scr["aqd"][rs, :], _bf(s))
            kv = _dot_tn(_blockdiag(scr["akd"][rs, :], cst["bd_rope"][...]), vbd)
            yield
            s = s * cst["ret_cdec"][...] + kv
            oscr[rs, 0:GROUP] = o
        scr["sa"][...] = s

    def gdn_p2():
        s = scr["sb"][...]
        for n, rs in enumerate(rows):
            u_v, w_k, qk = pb[n]
            r = _dot(jnp.concatenate([w_k, scr["bqg"][rs, :]], axis=0), _bf(s))
            yield
            u = u_v - r[0:CHUNK]
            ubd = _blockdiag(_bf(u), bd_std[...])
            o = r[CHUNK:2 * CHUNK] + _dot(qk, ubd)
            kv = _dot_tn(_blockdiag(scr["bkt"][rs, :], bd_std[...]), ubd)
            yield
            s = s * scr["bcd"][rs.start:rs.start + 1, :] + kv
            oscr[rs, GROUP:2 * GROUP] = o
        scr["sb"][...] = s

    def gl_p2(store, qx_ref, kx_ref, v_ref, cd_ref, bd_k, st_ref, col):
        st = st_ref[...]
        for n, rs in enumerate(rows):
            vbd = _blockdiag(v_ref[rs, :], bd_std[...])
            o = _dot(store[n], vbd) + _dot_nt(qx_ref[GL_EDGE, rs, :], _bf(st))
            kv = _dot_tn(vbd, _blockdiag(kx_ref[GL_EDGE, rs, :], bd_k[...]))
            yield
            st = st * cd_ref[rs.stop - 1:rs.stop, :] + kv
            oscr[rs, col:col + GROUP] = o
        st_ref[...] = st

    def finish(idx):
        inv_hd = 1.0 / HEAD_DIM
        gains = (None, par["gdn_nw"], par["gla_nw"], par["hg_nw"])
        gates = (Z_RG, Z_BG, Z_CG, Z_DG)
        for i in idx:
            o = oscr[:, i * GROUP:(i + 1) * GROUP]
            o = o * lax.rsqrt(seg_sum(o * o) * inv_hd + EPS)
            yield
            if gains[i] is not None:
                o = o * gains[i][...]
            o_ref[:, i * GROUP:(i + 1) * GROUP] = _bf(o * _silu(zcols(gates[i], GROUP)))

    _run_interleaved([g for n in range(n_chunks) for g in (gdn_p1(n), ret_p1(n))]
                     + [prep_c, prep_d])
    _run_interleaved([g for n in range(n_chunks)
                      for g in (gl_p1(n, scr["cqx"], scr["ckx"], cst["bd_gla"], pc),
                                gl_p1(n, scr["dqx"], scr["dkx"], bd_std, pd))]
                     + [gdn_p2(), ret_p2()])
    _run_interleaved([
        gl_p2(pc, scr["cqx"], scr["ckx"], scr["cv"], scr["ccd"], cst["bd_gla"], scr["sct"],
              2 * GROUP),
        gl_p2(pd, scr["dqx"], scr["dkx"], scr["dv"], scr["dcd"], bd_std, scr["sdt"],
              3 * GROUP),
        finish((0, 1))])
    _run_interleaved([finish((2, 3))])


def _mixer(zb, zf, cos_t, sin_t, params, batch, tb):
    m = zb.shape[0]
    t = m // batch
    nt = t // tb
    consts = {k: jnp.asarray(v, dt) for k, (v, dt) in _mixer_constants(tb).items()}
    const_args = [consts[k] for k in _CONST_ORDER]
    param_args = [params[k] for k in _PARAM_ORDER]

    def full_spec(a):
        nd = a.ndim
        return pl.BlockSpec(a.shape, lambda b, i, _nd=nd: (0,) * _nd)

    row_map = lambda b, i: (b * nt + i, 0)
    body = functools.partial(_mixer_body, n_chunks=tb // CHUNK)
    return pl.pallas_call(
        body,
        out_shape=jax.ShapeDtypeStruct((m, N_HEADS * GROUP), BF16),
        grid=(batch, nt),
        in_specs=[pl.BlockSpec((tb, Z_WIDTH), row_map),
                  pl.BlockSpec((tb, F_WIDTH), row_map),
                  pl.BlockSpec((tb, LANES), row_map),
                  pl.BlockSpec((tb, LANES), row_map)]
                 + [full_spec(a) for a in const_args]
                 + [full_spec(a) for a in param_args],
        out_specs=pl.BlockSpec((tb, N_HEADS * GROUP), row_map),
        scratch_shapes=_mixer_scratch(tb),
        compiler_params=pltpu.CompilerParams(
            dimension_semantics=("parallel", "arbitrary"),
            vmem_limit_bytes=VMEM_LIMIT),
        name="mixer",
    )(zb, zf, cos_t, sin_t, *const_args, *param_args)


def _in_proj_segments():
    src = {}
    off = 0
    for name, width in (("rq", 256), ("rk", 256), ("rv", 256), ("rg", 256),
                        ("bq", 256), ("bk", 256), ("bv", 256), ("ba", 4), ("bb", 4), ("bg", 256),
                        ("cq", 128), ("ck", 128), ("cv", 256), ("clr", 16), ("cg", 256),
                        ("dq", 256), ("df", 256), ("di", 256), ("dg", 256)):
        src[name] = off
        off += width

    def rope(name):
        return [(src[name] + h * HEAD_DIM + half * ROPE_HALF, ROPE_HALF)
                for half in range(2) for h in range(N_HEADS)]

    segs = rope("rq") + rope("rk")
    segs += [(src["rv"], 256), (src["rg"], 256), (src["bq"], 768), (src["bg"], 256),
             (src["cq"], 128), (src["ck"], 128), (src["cv"], 256), (src["cg"], 256),
             (src["dq"], 256), (src["di"], 256), (src["dg"], 256)]
    assert sum(w for _, w in segs) == Z_WIDTH
    segs += [(src["df"], 256), (src["ba"], 8), (src["clr"], GLA_LOWRANK),
             (None, LANES - SM_LR - GLA_LOWRANK)]
    assert sum(w for _, w in segs) == Z_WIDTH + F_WIDTH
    return segs


def kernel(x, mem, positions, mix_norm_w, w_in, gdn_conv_w, gdn_a_log, gdn_dt_bias, gdn_norm_w,
           gla_gk_up, gla_gk_bias, gla_norm_w, hgrn_lb_logits, hgrn_norm_w, w_out,
           xattn_norm_w, mem_norm_w, xattn_wq, xattn_wk, xattn_wv, xattn_wo,
           ffn_norm_w, ffn_up, ffn_conv_w, ffn_down, final_norm_w):
    batch, seq, d = x.shape
    depth = w_in.shape[0]
    m = batch * seq
    mem_len = mem.shape[1]

    w_in_p = jnp.concatenate(
        [jnp.zeros((depth, d, wd), BF16) if s0 is None else w_in[:, :, s0:s0 + wd].astype(BF16)
         for s0, wd in _in_proj_segments()], axis=2)

    freqs = ROPE_BASE ** (-jnp.arange(0, HEAD_DIM, 2, dtype=F32) / HEAD_DIM)
    freq_row = jnp.tile(freqs, N_HEADS).reshape(1, LANES)
    cos_t, sin_t = _rope_tables(positions.reshape(m, 1), freq_row, 1024)

    lb_all = jnp.cumsum(jax.nn.softmax(hgrn_lb_logits.astype(F32), axis=0), axis=0)
    lb_all = lb_all - lb_all[0]

    def head_row(v):
        return jnp.tile(v.astype(F32), N_HEADS).reshape(1, GROUP)

    h = x.reshape(m, d)
    mem2 = mem.reshape(batch * mem_len, d)
    for l in range(depth):
        gla_up = jnp.zeros((LANES, GLA_KEYS), F32).at[SM_LR:SM_LR + GLA_LOWRANK].set(gla_gk_up[l])
        params = {
            "gdn_conv": gdn_conv_w[l].astype(F32),
            "gdn_alog": jnp.repeat(gdn_a_log[l].astype(F32), HEAD_DIM).reshape(1, GROUP),
            "gdn_dt": jnp.repeat(gdn_dt_bias[l].astype(F32), HEAD_DIM).reshape(1, GROUP),
            "gdn_nw": head_row(gdn_norm_w[l]),
            "gla_up": gla_up,
            "gla_bias": gla_gk_bias[l].astype(F32).reshape(1, GLA_KEYS),
            "gla_nw": head_row(gla_norm_w[l]),
            "hg_loglb": jnp.log(lb_all[l]).reshape(1, GROUP),
            "hg_log1mlb": jnp.log1p(-lb_all[l]).reshape(1, GROUP),
            "hg_1mlb": (1.0 - lb_all[l]).reshape(1, GROUP),
            "hg_nw": head_row(hgrn_norm_w[l]),
        }
        zb, zf = _norm_matmul(h, mix_norm_w[l], w_in_p[l],
                              ((Z_WIDTH, BF16), (F_WIDTH, F32)), TM_IN_PROJ)
        o = _mixer(zb, zf, cos_t, sin_t, params, batch, TB_MIXER)

        w_kv = jnp.concatenate([xattn_wk[l], xattn_wv[l]], axis=1).astype(BF16)
        kv, = _norm_matmul(mem2, mem_norm_w, w_kv, ((2 * d, BF16),), TM_KV)
        h = _xattn(o, w_out[l].astype(BF16), h, xattn_norm_w[l], xattn_wq[l].astype(BF16),
                   kv.reshape(batch, mem_len, 2 * d), xattn_wo[l].astype(BF16), batch, TQ_XATTN)

        u, = _norm_matmul(h, ffn_norm_w[l], ffn_up[l].astype(BF16), ((ffn_up.shape[2], BF16),),
                          TM_FFN_UP)
        d_ff = ffn_down.shape[1]
        conv_w = ffn_conv_w[l].astype(F32) * jnp.where(jnp.arange(2 * d_ff) < d_ff, 0.5, 1.0)
        h = _conv_ffn_down(u, conv_w, ffn_down[l].astype(BF16), h,
                           final_norm_w, seq, TM_FFN_DOWN, l == depth - 1)
    return h.reshape(batch, seq, d)
```

```python
import functools

import numpy as np
import jax
import jax.numpy as jnp
from jax import lax
from jax.experimental import pallas as pl
from jax.experimental.pallas import tpu as pltpu

F32 = jnp.float32
BF16 = jnp.bfloat16

N_HEADS = 4
HEAD_DIM = 64
GROUP = N_HEADS * HEAD_DIM
GLA_KEY_DIM = 32
GLA_KEYS = N_HEADS * GLA_KEY_DIM
GLA_LOWRANK = 16
GLA_GATE_NORM = 16.0
GDN_CONV = 4
CHUNK = 64
ROPE_BASE = 10000.0
ROPE_HALF = HEAD_DIM // 2
XATTN_HEADS = 4
FFN_CONV = 3
CONV_PHASES = 4
EPS = 1e-6
LOG2E = 1.4426950408889634

LANES = 128
VMEM_LIMIT = 52 * 1024 * 1024

TM_IN_PROJ = 1024
TM_KV = 512
TQ_XATTN = 512
TM_FFN_UP = 512
TM_FFN_DOWN = 512
TB_MIXER = 256

Z_RQ, Z_RK, Z_RV, Z_RG = 0, 256, 512, 768
Z_BQKV, Z_BG = 1024, 1792
Z_CQ, Z_CK, Z_CV, Z_CG = 2048, 2176, 2304, 2560
Z_DQ, Z_DI, Z_DG = 2816, 3072, 3328
Z_WIDTH = 3584
F_DF, F_SMALL = 0, 256
F_WIDTH = 384
SM_BA, SM_BB, SM_LR = 0, 4, 8

LEVELS = (1, 2, 4, 8, 16, 32)
N_LEVELS = len(LEVELS)
GL_DIAG = N_LEVELS
GL_EDGE = N_LEVELS + 1


def _bf(x):
    return x.astype(BF16)


def _dot(a, b):
    return jnp.dot(a, b, preferred_element_type=F32)


def _dot_nt(a, b):
    return lax.dot_general(a, b, (((1,), (1,)), ((), ())), preferred_element_type=F32)


def _dot_tn(a, b):
    return lax.dot_general(a, b, (((0,), (0,)), ((), ())), preferred_element_type=F32)


def _split(x, n):
    parts = []
    r = x
    for i in range(n):
        p = r.astype(BF16)
        parts.append(p)
        if i < n - 1:
            r = r - p.astype(F32)
    return parts


def _sel_dot(c, x, n=2):
    acc = None
    for p in _split(x, n):
        t = _dot(c, p)
        acc = t if acc is None else acc + t
    return acc


def _dot_sel(x, c, n=2):
    acc = None
    for p in _split(x, n):
        t = _dot(p, c)
        acc = t if acc is None else acc + t
    return acc


def _dot_hp(a, b):
    a_hi, a_lo = _split(a, 2)
    b_hi, b_lo = _split(b, 2)
    return _dot(a_hi, b_hi) + _dot(a_lo, b_hi) + _dot(a_hi, b_lo)


def _silu(x):
    return x * jax.nn.sigmoid(x)


def _log_sigmoid(x):
    return jnp.minimum(x, 0.0) - jnp.log1p(jnp.exp(-jnp.abs(x)))


def _softplus(x):
    return jnp.maximum(x, 0.0) + jnp.log1p(jnp.exp(-jnp.abs(x)))


def _rms(x, w):
    ms = jnp.mean(x * x, axis=-1, keepdims=True)
    return x * lax.rsqrt(ms + EPS) * w


def _blockdiag(x, mask):
    return jnp.concatenate([x, x, x, x], axis=0) * mask


def _run_interleaved(gens):
    gens = list(gens)
    while gens:
        alive = []
        for g in gens:
            try:
                next(g)
                alive.append(g)
            except StopIteration:
                pass
        gens = alive


def _col_chunks(n, width):
    return [(c0, min(width, n - c0)) for c0 in range(0, n, width)]


def _norm_matmul_body(x_ref, nw_ref, w_ref, *o_refs, widths, col_chunk):
    xn = _bf(_rms(x_ref[...], nw_ref[...]))
    base = 0
    for o_ref, width in zip(o_refs, widths):
        for c0, cw in _col_chunks(width, col_chunk):
            o_ref[:, c0:c0 + cw] = _dot(xn, w_ref[:, base + c0:base + c0 + cw]).astype(o_ref.dtype)
        base += width


def _norm_matmul(x, nw, w, outs, tm):
    m, d = x.shape
    n = w.shape[1]
    widths = tuple(wd for wd, _ in outs)
    assert sum(widths) == n
    body = functools.partial(_norm_matmul_body, widths=widths, col_chunk=1024)
    res = pl.pallas_call(
        body,
        out_shape=tuple(jax.ShapeDtypeStruct((m, wd), dt) for wd, dt in outs),
        grid=(m // tm,),
        in_specs=[
            pl.BlockSpec((tm, d), lambda i: (i, 0)),
            pl.BlockSpec((1, d), lambda i: (0, 0)),
            pl.BlockSpec((d, n), lambda i: (0, 0), pipeline_mode=pl.Buffered(1)),
        ],
        out_specs=tuple(pl.BlockSpec((tm, wd), lambda i: (i, 0)) for wd, _ in outs),
        compiler_params=pltpu.CompilerParams(
            dimension_semantics=("parallel",),
            vmem_limit_bytes=VMEM_LIMIT),
        name="norm_matmul",
    )(x, nw.reshape(1, d), w)
    return res


def _rope_table_body(pos_ref, freq_ref, cos_ref, sin_ref):
    ang = pos_ref[...].astype(F32) * freq_ref[...]
    cos_ref[...] = jnp.cos(ang)
    sin_ref[...] = jnp.sin(ang)


def _rope_tables(pos, freq_row, tm):
    m = pos.shape[0]
    return pl.pallas_call(
        _rope_table_body,
        out_shape=(jax.ShapeDtypeStruct((m, LANES), F32),
                   jax.ShapeDtypeStruct((m, LANES), F32)),
        grid=(m // tm,),
        in_specs=[
            pl.BlockSpec((tm, 1), lambda i: (i, 0)),
            pl.BlockSpec((1, LANES), lambda i: (0, 0)),
        ],
        out_specs=(pl.BlockSpec((tm, LANES), lambda i: (i, 0)),
                   pl.BlockSpec((tm, LANES), lambda i: (i, 0))),
        compiler_params=pltpu.CompilerParams(dimension_semantics=("parallel",)),
        name="rope_tables",
    )(pos, freq_row)


def _xattn_body(mix_ref, wmix_ref, h_ref, nw_ref, wq_ref, k_ref, v_ref, wo_ref, o_ref, att_ref):
    h = h_ref[...] + _dot(mix_ref[...], wmix_ref[...])
    d = h.shape[-1]
    hd = d // XATTN_HEADS
    hn = _bf(_rms(h, nw_ref[...]))
    q = _bf(_dot(hn, wq_ref[...]) * (hd ** -0.5))
    for i in range(XATTN_HEADS):
        cs = slice(i * hd, (i + 1) * hd)
        s = _dot_nt(q[:, cs], k_ref[0, :, cs])
        s = s - jnp.max(s, axis=-1, keepdims=True)
        p = jnp.exp(s)
        p = p / jnp.sum(p, axis=-1, keepdims=True)
        att_ref[:, cs] = _bf(_dot(_bf(p), v_ref[0, :, cs]))
    o_ref[...] = h + _dot(att_ref[...], wo_ref[...])


def _xattn(mix, w_mix, h, nw, wq, kv, wo, batch, tq):
    m, d = h.shape
    t = m // batch
    mem_len = kv.shape[1]
    nt = t // tq
    row_map = lambda b, i: (b * nt + i, 0)
    const_map = lambda b, i: (0, 0)
    weight = lambda: pl.BlockSpec((d, d), const_map, pipeline_mode=pl.Buffered(1))
    return pl.pallas_call(
        _xattn_body,
        out_shape=jax.ShapeDtypeStruct((m, d), F32),
        grid=(batch, nt),
        in_specs=[
            pl.BlockSpec((tq, d), row_map),
            weight(),
            pl.BlockSpec((tq, d), row_map),
            pl.BlockSpec((1, d), const_map),
            weight(),
            pl.BlockSpec((1, mem_len, d), lambda b, i: (b, 0, 0)),
            pl.BlockSpec((1, mem_len, d), lambda b, i: (b, 0, 1)),
            weight(),
        ],
        out_specs=pl.BlockSpec((tq, d), row_map),
        scratch_shapes=[pltpu.VMEM((tq, d), BF16)],
        compiler_params=pltpu.CompilerParams(
            dimension_semantics=("parallel", "parallel"),
            vmem_limit_bytes=VMEM_LIMIT),
        name="xattn",
    )(mix, w_mix, h, nw.reshape(1, d), wq, kv, kv, wo)


def _shift_rows(x, k, tail, row):
    y = pltpu.roll(x, k, 0)
    head = y[0:8]
    for j in range(k):
        src = tail.shape[0] - k + j
        head = jnp.where(row[0:8] == j, tail[src:src + 1, :], head)
    return jnp.concatenate([head, y[8:]], axis=0)


def _conv_ffn_down_body(u_ref, halo_ref, cw_ref, wd_ref, r_ref, fw_ref, o_ref,
                        ext_ref, yst_ref, act_ref, *, tiles_per_seq, final_norm):
    tm = u_ref.shape[0]
    d_ff = act_ref.shape[1]
    first = (pl.program_id(0) % tiles_per_seq) == 0
    keep = jnp.where(first, 0.0, 1.0)
    hrows = halo_ref.shape[0]
    nq = tm // CONV_PHASES

    def phase_rows(half, off):
        return ext_ref[half, pl.ds(8 + off, nq, stride=CONV_PHASES), :]

    for c0 in range(0, d_ff, LANES):
        for half in range(2):
            cs = slice(half * d_ff + c0, half * d_ff + c0 + LANES)
            ext_ref[half, 0:8, :] = halo_ref[hrows - 8:hrows, cs].astype(F32) * keep
            ext_ref[half, 8:8 + tm, :] = u_ref[:, cs].astype(F32)
        xs = [[phase_rows(half, off) for off in range(1 - FFN_CONV, CONV_PHASES)]
              for half in range(2)]
        ws = [cw_ref[:, half * d_ff + c0:half * d_ff + c0 + LANES] for half in range(2)]
        for p in range(CONV_PHASES):
            ys = []
            for half in range(2):
                y = None
                for k in range(FFN_CONV):
                    t = xs[half][p + k] * ws[half][k:k + 1, :]
                    y = t if y is None else y + t
                ys.append(y)
            hg = ys[0]
            yst_ref[pl.ds(p, nq, stride=CONV_PHASES), :] = (hg + hg * jnp.tanh(hg)) * ys[1]
        act_ref[:, c0:c0 + LANES] = _bf(yst_ref[...])
    out = r_ref[...] + _dot(act_ref[...], wd_ref[...])
    if final_norm:
        out = _rms(out, fw_ref[...])
    o_ref[...] = out


def _conv_ffn_down(u, conv_w, w_down, res, final_w, seq_len, tm, final_norm):
    m, n2 = u.shape
    d_ff = n2 // 2
    d = w_down.shape[1]
    halo = 16
    body = functools.partial(_conv_ffn_down_body, tiles_per_seq=seq_len // tm,
                             final_norm=final_norm)
    return pl.pallas_call(
        body,
        out_shape=jax.ShapeDtypeStruct((m, d), F32),
        grid=(m // tm,),
        in_specs=[
            pl.BlockSpec((tm, n2), lambda i: (i, 0)),
            pl.BlockSpec((halo, n2), lambda i: (jnp.maximum(i * (tm // halo) - 1, 0), 0)),
            pl.BlockSpec((FFN_CONV, n2), lambda i: (0, 0)),
            pl.BlockSpec((d_ff, d), lambda i: (0, 0), pipeline_mode=pl.Buffered(1)),
            pl.BlockSpec((tm, d), lambda i: (i, 0)),
            pl.BlockSpec((1, d), lambda i: (0, 0)),
        ],
        out_specs=pl.BlockSpec((tm, d), lambda i: (i, 0)),
        scratch_shapes=[pltpu.VMEM((2, 8 + tm, LANES), F32),
                        pltpu.VMEM((tm, LANES), F32),
                        pltpu.VMEM((tm, d_ff), BF16)],
        compiler_params=pltpu.CompilerParams(
            dimension_semantics=("parallel",),
            vmem_limit_bytes=VMEM_LIMIT),
        name="conv_ffn_down",
    )(u, u, conv_w, w_down, res, final_w.reshape(1, d))


def _mixer_constants(tb):
    c = CHUNK
    ar = np.arange
    std_head = ar(GROUP) // HEAD_DIM
    rope_head = (ar(GROUP) % LANES) // ROPE_HALF
    gla_head = ar(GLA_KEYS) // GLA_KEY_DIM
    row_head = ar(GROUP) // c
    s_of = ar(GROUP) % c
    t = ar(c)
    tt = ar(tb) % c
    same_chunk = (ar(tb)[:, None] // c) == (ar(tb)[None, :] // c)

    out = {}
    bd_std = row_head[:, None] == std_head[None, :]
    out["bd_std"] = (bd_std, BF16)
    out["bd_std2"] = (np.concatenate([bd_std, bd_std], axis=1), BF16)
    out["bd_rope"] = (row_head[:, None] == rope_head[None, :], BF16)
    out["bd_gla"] = (row_head[:, None] == gla_head[None, :], BF16)

    log_gamma = np.log(1.0 - np.exp2(-5.0 - ar(N_HEADS, dtype=np.float64)))
    scale = HEAD_DIM ** -0.5
    rel = t[:, None] - s_of[None, :]
    lg_w = log_gamma[std_head][None, :]
    out["ret_d"] = (np.where(rel >= 0, np.exp(lg_w * np.maximum(rel, 0)), 0.0) * scale, F32)
    lg_r = log_gamma[rope_head][None, :]
    out["ret_qdec"] = (np.exp(lg_r * (tt[:, None] + 1.0)), F32)
    out["ret_kdec"] = (np.exp(lg_r * (c - 1.0 - tt[:, None])) * scale, F32)
    out["ret_cdec"] = (np.exp(log_gamma[std_head] * c)[None, :], F32)

    lvl, sel_q, sel_k = [], [], []
    for b in LEVELS:
        blk = t // (2 * b)
        second = (t % (2 * b)) >= b
        bound = blk * 2 * b + b - 1
        msk = (blk[:, None] == blk[None, :]) & second[:, None] & (~second)[None, :]
        lvl.append(msk[:, s_of])
        sel_q.append(second[:, None] & (t[None, :] > bound[:, None]) & (t[None, :] <= t[:, None]))
        sel_k.append((~second)[:, None] & (t[None, :] > t[:, None]) & (t[None, :] <= bound[:, None]))
    lvl.append(t[:, None] == s_of[None, :])
    out["lvl"] = (np.stack(lvl), F32)
    tri = t[None, :] <= t[:, None]
    tail = t[None, :] > t[:, None]
    eye_b = np.eye(tb // c, dtype=bool)
    sel = [np.kron(eye_b, s_) for s_ in [a | b for a, b in zip(sel_q, sel_k)] + [tri, tail]]
    out["sel"] = (np.stack(sel), BF16)
    out["ones_b"] = (same_chunk, BF16)
    out["trit_w"] = (tt[:, None] <= s_of[None, :], F32)
    out["causal_w"] = (tt[:, None] >= s_of[None, :], F32)
    out["strict_w"] = (tt[:, None] > s_of[None, :], F32)
    out["eye_w"] = (t[:, None] == s_of[None, :], F32)
    eab = np.zeros((LANES, 2 * GROUP))
    eab[SM_BA + std_head, ar(GROUP)] = 1.0
    eab[SM_BB + std_head, GROUP + ar(GROUP)] = 1.0
    out["exp_ab"] = (eab, BF16)
    return out


_CONST_ORDER = ("bd_std", "bd_std2", "bd_rope", "bd_gla", "ret_d", "ret_qdec", "ret_kdec",
                "ret_cdec", "lvl", "sel", "ones_b", "trit_w", "causal_w", "strict_w", "eye_w",
                "exp_ab")
_PARAM_ORDER = ("gdn_conv", "gdn_alog", "gdn_dt", "gdn_nw", "gla_up", "gla_bias", "gla_nw",
                "hg_loglb", "hg_log1mlb", "hg_1mlb", "hg_nw")
_SCRATCH_ORDER = ("sa", "sb", "sct", "sdt", "tail",
                  "aq", "aqd", "ak", "akd", "av",
                  "bq", "bk", "bbr", "brc", "brhs", "bqg", "bkt", "bcd",
                  "cqx", "ckx", "cv", "ccd", "dqx", "dkx", "dv", "dcd", "oscr")


def _mixer_scratch(tb):
    f, b = F32, BF16
    shapes = {
        "sa": ((GROUP, GROUP), f), "sb": ((GROUP, GROUP), f),
        "sct": ((GROUP, GLA_KEYS), f), "sdt": ((GROUP, GROUP), f),
        "tail": ((8, 3 * GROUP), f),
        "aq": ((tb, GROUP), b), "aqd": ((tb, GROUP), b), "ak": ((tb, GROUP), b),
        "akd": ((tb, GROUP), b), "av": ((tb, GROUP), b),
        "bq": ((tb, GROUP), b), "bk": ((tb, GROUP), b), "bbr": ((tb, GROUP), f),
        "brc": ((tb, GROUP), f), "brhs": ((tb, 2 * GROUP), b), "bqg": ((tb, GROUP), b),
        "bkt": ((tb, GROUP), b), "bcd": ((tb, GROUP), f),
        "cqx": ((N_LEVELS + 2, tb, GLA_KEYS), b), "ckx": ((N_LEVELS + 2, tb, GLA_KEYS), b),
        "cv": ((tb, GROUP), b), "ccd": ((tb, GLA_KEYS), f),
        "dqx": ((N_LEVELS + 2, tb, GROUP), b), "dkx": ((N_LEVELS + 2, tb, GROUP), b),
        "dv": ((tb, GROUP), b), "dcd": ((tb, GROUP), f),
        "oscr": ((tb, N_HEADS * GROUP), f),
    }
    return [pltpu.VMEM(*shapes[k]) for k in _SCRATCH_ORDER]


def _mixer_body(*refs, n_chunks):
    zb_ref, zf_ref, cos_ref, sin_ref = refs[:4]
    nc, npar = len(_CONST_ORDER), len(_PARAM_ORDER)
    cst = dict(zip(_CONST_ORDER, refs[4:4 + nc]))
    par = dict(zip(_PARAM_ORDER, refs[4 + nc:4 + nc + npar]))
    o_ref = refs[4 + nc + npar]
    scr = dict(zip(_SCRATCH_ORDER, refs[5 + nc + npar:]))
    tb = zb_ref.shape[0]
    rows = [slice(n * CHUNK, (n + 1) * CHUNK) for n in range(n_chunks)]
    bd_std = cst["bd_std"]

    def zcols(col, width):
        return zb_ref[:, col:col + width].astype(F32)

    @pl.when(pl.program_id(1) == 0)
    def _():
        for k in ("sa", "sb", "sct", "sdt", "tail"):
            scr[k][...] = jnp.zeros_like(scr[k])

    def seg_sum(x):
        return _dot(_bf(x), bd_std[...])

    cos = cos_ref[...]
    sin = sin_ref[...]

    def rope(col):
        x1 = zcols(col, LANES)
        x2 = zcols(col + LANES, LANES)
        return jnp.concatenate([x1 * cos - x2 * sin, x1 * sin + x2 * cos], axis=1)

    qr = rope(Z_RQ)
    scr["aq"][...] = _bf(qr)
    scr["aqd"][...] = _bf(qr * cst["ret_qdec"][...])
    kr = rope(Z_RK)
    scr["ak"][...] = _bf(kr)
    scr["akd"][...] = _bf(kr * cst["ret_kdec"][...])
    scr["av"][...] = zb_ref[:, Z_RV:Z_RV + GROUP]

    row = lax.broadcasted_iota(jnp.int32, (tb, 1), 0)
    x = zcols(Z_BQKV, 3 * GROUP)
    tail = scr["tail"][...]
    cw = par["gdn_conv"][...]
    y = x * cw[GDN_CONV - 1:GDN_CONV, :]
    for k in range(1, GDN_CONV):
        y = y + _shift_rows(x, k, tail, row) * cw[GDN_CONV - 1 - k:GDN_CONV - k, :]
    scr["tail"][...] = x[tb - 8:tb, :]
    qkv = _silu(y)
    qb, kb, vv = qkv[:, 0:GROUP], qkv[:, GROUP:2 * GROUP], qkv[:, 2 * GROUP:3 * GROUP]
    sm = zf_ref[:, F_SMALL:F_SMALL + LANES]
    bab = _dot_sel(sm, cst["exp_ab"][...])
    g = (-LOG2E * jnp.exp(par["gdn_alog"][...])) * _softplus(bab[:, 0:GROUP] + par["gdn_dt"][...])
    beta = jax.nn.sigmoid(bab[:, GROUP:2 * GROUP])
    qn = qb * lax.rsqrt(seg_sum(qb * qb) + EPS) * (HEAD_DIM ** -0.5)
    kn = kb * lax.rsqrt(seg_sum(kb * kb) + EPS)
    ones_b = cst["ones_b"][...]
    gc = _sel_dot(cst["sel"][N_LEVELS], g)
    gcr = _sel_dot(ones_b, g * cst["trit_w"][...])
    glast = _sel_dot(ones_b, g)
    dec = jnp.exp2(gc - gcr)
    gam = jnp.exp2(gc)
    scr["bq"][...] = _bf(qn)
    scr["bk"][...] = _bf(kn)
    scr["bbr"][...] = beta * jnp.where(cst["strict_w"][...] > 0, dec, 0.0)
    scr["brc"][...] = jnp.where(cst["causal_w"][...] > 0, dec, 0.0)
    scr["brhs"][...] = _bf(jnp.concatenate([beta * vv, beta * gam * kn], axis=1))
    scr["bqg"][...] = _bf(qn * gam)
    scr["bkt"][...] = _bf(kn * jnp.exp2(glast - gc))
    scr["bcd"][...] = jnp.exp2(glast)

    def gl_prep(q, k, v, logf, qx_ref, kx_ref, v_ref, cd_ref):
        hi, lo = _split(logf * LOG2E, 2)
        for j in range(N_LEVELS + 2):
            sel = cst["sel"][j]
            e = jnp.exp2(_dot(sel, hi) + _dot(sel, lo))
            if j < N_LEVELS:
                qx_ref[j] = _bf(q * e)
                kx_ref[j] = _bf(k * e)
            elif j == N_LEVELS:
                qx_ref[GL_EDGE] = _bf(q * e)
                cd_ref[...] = e
            else:
                kx_ref[GL_EDGE] = _bf(k * e)
            yield
        qx_ref[GL_DIAG] = _bf(q)
        kx_ref[GL_DIAG] = _bf(k)
        v_ref[...] = _bf(v)

    logit = _dot_hp(sm, par["gla_up"][...]) + par["gla_bias"][...]
    prep_c = gl_prep(zcols(Z_CQ, GLA_KEYS) * (GLA_KEY_DIM ** -0.5),
                     zcols(Z_CK, GLA_KEYS), zb_ref[:, Z_CV:Z_CV + GROUP],
                     _log_sigmoid(logit) * (1.0 / GLA_GATE_NORM),
                     scr["cqx"], scr["ckx"], scr["cv"], scr["ccd"])

    f_pre = zf_ref[:, F_DF:F_DF + GROUP]
    a = par["hg_loglb"][...]
    b = par["hg_log1mlb"][...] + _log_sigmoid(f_pre)
    logf_d = jnp.maximum(a, b) + jnp.log1p(jnp.exp(-jnp.abs(a - b)))
    prep_d = gl_prep(zcols(Z_DQ, GROUP), par["hg_1mlb"][...] * jax.nn.sigmoid(-f_pre),
                     zb_ref[:, Z_DI:Z_DI + GROUP], logf_d,
                     scr["dqx"], scr["dkx"], scr["dv"], scr["dcd"])

    pa, pb, pc, pd = {}, {}, {}, {}

    def ret_p1(n):
        rs = rows[n]
        p = _dot_nt(scr["aq"][rs, :], _blockdiag(scr["ak"][rs, :], cst["bd_rope"][...]))
        yield
        pa[n] = _bf(p * cst["ret_d"][...])

    def gl_p1(n, qx_ref, kx_ref, bd_k, store):
        rs = rows[n]
        p = None
        for l in range(N_LEVELS + 1):
            t = _dot_nt(qx_ref[l, rs, :], _blockdiag(kx_ref[l, rs, :], bd_k[...]))
            yield
            t = t * cst["lvl"][l]
            p = t if p is None else p + t
        store[n] = _bf(p)

    def wide_matmul(a, b):
        return _dot(_bf(a), _blockdiag(_bf(b), bd_std[...]))

    def gdn_p1(n):
        rs = rows[n]
        knb = scr["bk"][rs, :]
        kbd = _blockdiag(knb, bd_std[...])
        kk = _dot_nt(knb, kbd)
        qk = _dot_nt(scr["bq"][rs, :], kbd)
        yield
        lmat = scr["bbr"][rs, :] * kk
        tinv = cst["eye_w"][...] - lmat * cst["lvl"][0]
        for li in range(1, N_LEVELS):
            xb = wide_matmul(lmat * cst["lvl"][li], tinv)
            yield
            tinv = tinv - wide_matmul(tinv, xb)
            yield
        uw = _dot(_bf(tinv), _blockdiag(scr["brhs"][rs, :], cst["bd_std2"][...]))
        yield
        pb[n] = (uw[:, 0:GROUP], _bf(uw[:, GROUP:2 * GROUP]), _bf(qk * scr["brc"][rs, :]))

    oscr = scr["oscr"]

    def ret_p2():
        s = scr["sa"][...]
        for n, rs in enumerate(rows):
            vbd = _blockdiag(scr["av"][rs, :], bd_std[...])
            o = _dot(pa[n], vbd) + _dot(scr["aqd"][rs, :], _bf(s))
            kv = _dot_tn(_blockdiag(scr["akd"][rs, :], cst["bd_rope"][...]), vbd)
            yield
            s = s * cst["ret_cdec"][...] + kv
            oscr[rs, 0:GROUP] = o
        scr["sa"][...] = s

    def gdn_p2():
        s = scr["sb"][...]
        for n, rs in enumerate(rows):
            u_v, w_k, qk = pb[n]
            r = _dot(jnp.concatenate([w_k, scr["bqg"][rs, :]], axis=0), _bf(s))
            yield
            u = u_v - r[0:CHUNK]
            ubd = _blockdiag(_bf(u), bd_std[...])
            o = r[CHUNK:2 * CHUNK] + _dot(qk, ubd)
            kv = _dot_tn(_blockdiag(scr["bkt"][rs, :], bd_std[...]), ubd)
            yield
            s = s * scr["bcd"][rs.start:rs.start + 1, :] + kv
            oscr[rs, GROUP:2 * GROUP] = o
        scr["sb"][...] = s

    def gl_p2(store, qx_ref, kx_ref, v_ref, cd_ref, bd_k, st_ref, col):
        st = st_ref[...]
        for n, rs in enumerate(rows):
            vbd = _blockdiag(v_ref[rs, :], bd_std[...])
            o = _dot(store[n], vbd) + _dot_nt(qx_ref[GL_EDGE, rs, :], _bf(st))
            kv = _dot_tn(vbd, _blockdiag(kx_ref[GL_EDGE, rs, :], bd_k[...]))
            yield
            st = st * cd_ref[rs.stop - 1:rs.stop, :] + kv
            oscr[rs, col:col + GROUP] = o
        st_ref[...] = st

    def finish(idx):
        inv_hd = 1.0 / HEAD_DIM
        gains = (None, par["gdn_nw"], par["gla_nw"], par["hg_nw"])
        gates = (Z_RG, Z_BG, Z_CG, Z_DG)
        for i in idx:
            o = oscr[:, i * GROUP:(i + 1) * GROUP]
            o = o * lax.rsqrt(seg_sum(o * o) * inv_hd + EPS)
            yield
            if gains[i] is not None:
                o = o * gains[i][...]
            o_ref[:, i * GROUP:(i + 1) * GROUP] = _bf(o * _silu(zcols(gates[i], GROUP)))

    _run_interleaved([g for n in range(n_chunks) for g in (gdn_p1(n), ret_p1(n))]
                     + [prep_c, prep_d])
    _run_interleaved([g for n in range(n_chunks)
                      for g in (gl_p1(n, scr["cqx"], scr["ckx"], cst["bd_gla"], pc),
                                gl_p1(n, scr["dqx"], scr["dkx"], bd_std, pd))]
                     + [gdn_p2(), ret_p2()])
    _run_interleaved([
        gl_p2(pc, scr["cqx"], scr["ckx"], scr["cv"], scr["ccd"], cst["bd_gla"], scr["sct"],
              2 * GROUP),
        gl_p2(pd, scr["dqx"], scr["dkx"], scr["dv"], scr["dcd"], bd_std, scr["sdt"],
              3 * GROUP),
        finish((0, 1))])
    _run_interleaved([finish((2, 3))])


def _mixer(zb, zf, cos_t, sin_t, params, batch, tb):
    m = zb.shape[0]
    t = m // batch
    nt = t // tb
    consts = {k: jnp.asarray(v, dt) for k, (v, dt) in _mixer_constants(tb).items()}
    const_args = [consts[k] for k in _CONST_ORDER]
    param_args = [params[k] for k in _PARAM_ORDER]

    def full_spec(a):
        nd = a.ndim
        return pl.BlockSpec(a.shape, lambda b, i, _nd=nd: (0,) * _nd)

    row_map = lambda b, i: (b * nt + i, 0)
    body = functools.partial(_mixer_body, n_chunks=tb // CHUNK)
    return pl.pallas_call(
        body,
        out_shape=jax.ShapeDtypeStruct((m, N_HEADS * GROUP), BF16),
        grid=(batch, nt),
        in_specs=[pl.BlockSpec((tb, Z_WIDTH), row_map),
                  pl.BlockSpec((tb, F_WIDTH), row_map),
                  pl.BlockSpec((tb, LANES), row_map),
                  pl.BlockSpec((tb, LANES), row_map)]
                 + [full_spec(a) for a in const_args]
                 + [full_spec(a) for a in param_args],
        out_specs=pl.BlockSpec((tb, N_HEADS * GROUP), row_map),
        scratch_shapes=_mixer_scratch(tb),
        compiler_params=pltpu.CompilerParams(
            dimension_semantics=("parallel", "arbitrary"),
            vmem_limit_bytes=VMEM_LIMIT),
        name="mixer",
    )(zb, zf, cos_t, sin_t, *const_args, *param_args)


def _in_proj_segments():
    src = {}
    off = 0
    for name, width in (("rq", 256), ("rk", 256), ("rv", 256), ("rg", 256),
                        ("bq", 256), ("bk", 256), ("bv", 256), ("ba", 4), ("bb", 4), ("bg", 256),
                        ("cq", 128), ("ck", 128), ("cv", 256), ("clr", 16), ("cg", 256),
                        ("dq", 256), ("df", 256), ("di", 256), ("dg", 256)):
        src[name] = off
        off += width

    def rope(name):
        return [(src[name] + h * HEAD_DIM + half * ROPE_HALF, ROPE_HALF)
                for half in range(2) for h in range(N_HEADS)]

    segs = rope("rq") + rope("rk")
    segs += [(src["rv"], 256), (src["rg"], 256), (src["bq"], 768), (src["bg"], 256),
             (src["cq"], 128), (src["ck"], 128), (src["cv"], 256), (src["cg"], 256),
             (src["dq"], 256), (src["di"], 256), (src["dg"], 256)]
    assert sum(w for _, w in segs) == Z_WIDTH
    segs += [(src["df"], 256), (src["ba"], 8), (src["clr"], GLA_LOWRANK),
             (None, LANES - SM_LR - GLA_LOWRANK)]
    assert sum(w for _, w in segs) == Z_WIDTH + F_WIDTH
    return segs


def kernel(x, mem, positions, mix_norm_w, w_in, gdn_conv_w, gdn_a_log, gdn_dt_bias, gdn_norm_w,
           gla_gk_up, gla_gk_bias, gla_norm_w, hgrn_lb_logits, hgrn_norm_w, w_out,
           xattn_norm_w, mem_norm_w, xattn_wq, xattn_wk, xattn_wv, xattn_wo,
           ffn_norm_w, ffn_up, ffn_conv_w, ffn_down, final_norm_w):
    batch, seq, d = x.shape
    depth = w_in.shape[0]
    m = batch * seq
    mem_len = mem.shape[1]

    w_in_p = jnp.concatenate(
        [jnp.zeros((depth, d, wd), BF16) if s0 is None else w_in[:, :, s0:s0 + wd].astype(BF16)
         for s0, wd in _in_proj_segments()], axis=2)

    freqs = ROPE_BASE ** (-jnp.arange(0, HEAD_DIM, 2, dtype=F32) / HEAD_DIM)
    freq_row = jnp.tile(freqs, N_HEADS).reshape(1, LANES)
    cos_t, sin_t = _rope_tables(positions.reshape(m, 1), freq_row, 1024)

    lb_all = jnp.cumsum(jax.nn.softmax(hgrn_lb_logits.astype(F32), axis=0), axis=0)
    lb_all = lb_all - lb_all[0]

    def head_row(v):
        return jnp.tile(v.astype(F32), N_HEADS).reshape(1, GROUP)

    h = x.reshape(m, d)
    mem2 = mem.reshape(batch * mem_len, d)
    for l in range(depth):
        gla_up = jnp.zeros((LANES, GLA_KEYS), F32).at[SM_LR:SM_LR + GLA_LOWRANK].set(gla_gk_up[l])
        params = {
            "gdn_conv": gdn_conv_w[l].astype(F32),
            "gdn_alog": jnp.repeat(gdn_a_log[l].astype(F32), HEAD_DIM).reshape(1, GROUP),
            "gdn_dt": jnp.repeat(gdn_dt_bias[l].astype(F32), HEAD_DIM).reshape(1, GROUP),
            "gdn_nw": head_row(gdn_norm_w[l]),
            "gla_up": gla_up,
            "gla_bias": gla_gk_bias[l].astype(F32).reshape(1, GLA_KEYS),
            "gla_nw": head_row(gla_norm_w[l]),
            "hg_loglb": jnp.log(lb_all[l]).reshape(1, GROUP),
            "hg_log1mlb": jnp.log1p(-lb_all[l]).reshape(1, GROUP),
            "hg_1mlb": (1.0 - lb_all[l]).reshape(1, GROUP),
            "hg_nw": head_row(hgrn_norm_w[l]),
        }
        zb, zf = _norm_matmul(h, mix_norm_w[l], w_in_p[l],
                              ((Z_WIDTH, BF16), (F_WIDTH, F32)), TM_IN_PROJ)
        o = _mixer(zb, zf, cos_t, sin_t, params, batch, TB_MIXER)

        w_kv = jnp.concatenate([xattn_wk[l], xattn_wv[l]], axis=1).astype(BF16)
        kv, = _norm_matmul(mem2, mem_norm_w, w_kv, ((2 * d, BF16),), TM_KV)
        h = _xattn(o, w_out[l].astype(BF16), h, xattn_norm_w[l], xattn_wq[l].astype(BF16),
                   kv.reshape(batch, mem_len, 2 * d), xattn_wo[l].astype(BF16), batch, TQ_XATTN)

        u, = _norm_matmul(h, ffn_norm_w[l], ffn_up[l].astype(BF16), ((ffn_up.shape[2], BF16),),
                          TM_FFN_UP)
        d_ff = ffn_down.shape[1]
        conv_w = ffn_conv_w[l].astype(F32) * jnp.where(jnp.arange(2 * d_ff) < d_ff, 0.5, 1.0)
        h = _conv_ffn_down(u, conv_w, ffn_down[l].astype(BF16), h,
                           final_norm_w, seq, TM_FFN_DOWN, l == depth - 1)
    return h.reshape(batch, seq, d)
```

```python
import functools

import numpy as np
import jax
import jax.numpy as jnp
from jax import lax
from jax.experimental import pallas as pl
from jax.experimental.pallas import tpu as pltpu

F32 = jnp.float32
BF16 = jnp.bfloat16

N_HEADS = 4
HEAD_DIM = 64
GROUP = N_HEADS * HEAD_DIM
GLA_KEY_DIM = 32
GLA_KEYS = N_HEADS * GLA_KEY_DIM
GLA_LOWRANK = 16
GLA_GATE_NORM = 16.0
GDN_CONV = 4
CHUNK = 64
ROPE_BASE = 10000.0
ROPE_HALF = HEAD_DIM // 2
XATTN_HEADS = 4
FFN_CONV = 3
CONV_PHASES = 4
EPS = 1e-6
LOG2E = 1.4426950408889634

LANES = 128
VMEM_LIMIT = 52 * 1024 * 1024

TM_IN_PROJ = 1024
TM_KV = 1024
TQ_XATTN = 1024
TM_FFN_UP = 1024
TM_FFN_DOWN = 512
TB_MIXER = 256

Z_RQ, Z_RK, Z_RV, Z_RG = 0, 256, 512, 768
Z_BQKV, Z_BG = 1024, 1792
Z_CQ, Z_CK, Z_CV, Z_CG = 2048, 2176, 2304, 2560
Z_DQ, Z_DI, Z_DG = 2816, 3072, 3328
Z_WIDTH = 3584
F_DF, F_SMALL = 0, 256
F_WIDTH = 384
SM_BA, SM_BB, SM_LR = 0, 4, 8

LEVELS = (1, 2, 4, 8, 16, 32)
N_LEVELS = len(LEVELS)
GL_DIAG = N_LEVELS
GL_EDGE = N_LEVELS + 1


def _bf(x):
    return x.astype(BF16)


def _dot(a, b):
    return jnp.dot(a, b, preferred_element_type=F32)


def _dot_nt(a, b):
    return lax.dot_general(a, b, (((1,), (1,)), ((), ())), preferred_element_type=F32)


def _dot_tn(a, b):
    return lax.dot_general(a, b, (((0,), (0,)), ((), ())), preferred_element_type=F32)


def _split(x, n):
    parts = []
    r = x
    for i in range(n):
        p = r.astype(BF16)
        parts.append(p)
        if i < n - 1:
            r = r - p.astype(F32)
    return parts


def _sel_dot(c, x, n=2):
    acc = None
    for p in _split(x, n):
        t = _dot(c, p)
        acc = t if acc is None else acc + t
    return acc


def _dot_sel(x, c, n=2):
    acc = None
    for p in _split(x, n):
        t = _dot(p, c)
        acc = t if acc is None else acc + t
    return acc


def _dot_hp(a, b):
    a_hi, a_lo = _split(a, 2)
    b_hi, b_lo = _split(b, 2)
    return _dot(a_hi, b_hi) + _dot(a_lo, b_hi) + _dot(a_hi, b_lo)


def _silu(x):
    return x * jax.nn.sigmoid(x)


def _log_sigmoid(x):
    return jnp.minimum(x, 0.0) - jnp.log1p(jnp.exp(-jnp.abs(x)))


def _softplus(x):
    return jnp.maximum(x, 0.0) + jnp.log1p(jnp.exp(-jnp.abs(x)))


def _rms(x, w):
    ms = jnp.mean(x * x, axis=-1, keepdims=True)
    return x * lax.rsqrt(ms + EPS) * w


def _blockdiag(x, mask):
    return jnp.concatenate([x, x, x, x], axis=0) * mask


def _run_interleaved(gens):
    gens = list(gens)
    while gens:
        alive = []
        for g in gens:
            try:
                next(g)
                alive.append(g)
            except StopIteration:
                pass
        gens = alive


def _col_chunks(n, width):
    return [(c0, min(width, n - c0)) for c0 in range(0, n, width)]


def _norm_matmul_body(x_ref, nw_ref, w_ref, *o_refs, widths, col_chunk):
    xn = _bf(_rms(x_ref[...], nw_ref[...]))
    base = 0
    for o_ref, width in zip(o_refs, widths):
        for c0, cw in _col_chunks(width, col_chunk):
            o_ref[:, c0:c0 + cw] = _dot(xn, w_ref[:, base + c0:base + c0 + cw]).astype(o_ref.dtype)
        base += width


def _norm_matmul(x, nw, w, outs, tm):
    m, d = x.shape
    n = w.shape[1]
    widths = tuple(wd for wd, _ in outs)
    assert sum(widths) == n
    body = functools.partial(_norm_matmul_body, widths=widths, col_chunk=1024)
    res = pl.pallas_call(
        body,
        out_shape=tuple(jax.ShapeDtypeStruct((m, wd), dt) for wd, dt in outs),
        grid=(m // tm,),
        in_specs=[
            pl.BlockSpec((tm, d), lambda i: (i, 0)),
            pl.BlockSpec((1, d), lambda i: (0, 0)),
            pl.BlockSpec((d, n), lambda i: (0, 0), pipeline_mode=pl.Buffered(1)),
        ],
        out_specs=tuple(pl.BlockSpec((tm, wd), lambda i: (i, 0)) for wd, _ in outs),
        compiler_params=pltpu.CompilerParams(
            dimension_semantics=("parallel",),
            vmem_limit_bytes=VMEM_LIMIT),
        name="norm_matmul",
    )(x, nw.reshape(1, d), w)
    return res


def _rope_table_body(pos_ref, freq_ref, cos_ref, sin_ref):
    ang = pos_ref[...].astype(F32) * freq_ref[...]
    cos_ref[...] = jnp.cos(ang)
    sin_ref[...] = jnp.sin(ang)


def _rope_tables(pos, freq_row, tm):
    m = pos.shape[0]
    return pl.pallas_call(
        _rope_table_body,
        out_shape=(jax.ShapeDtypeStruct((m, LANES), F32),
                   jax.ShapeDtypeStruct((m, LANES), F32)),
        grid=(m // tm,),
        in_specs=[
            pl.BlockSpec((tm, 1), lambda i: (i, 0)),
            pl.BlockSpec((1, LANES), lambda i: (0, 0)),
        ],
        out_specs=(pl.BlockSpec((tm, LANES), lambda i: (i, 0)),
                   pl.BlockSpec((tm, LANES), lambda i: (i, 0))),
        compiler_params=pltpu.CompilerParams(dimension_semantics=("parallel",)),
        name="rope_tables",
    )(pos, freq_row)


def _xattn_body(mix_ref, wmix_ref, h_ref, nw_ref, wq_ref, k_ref, v_ref, wo_ref, o_ref, att_ref):
    h = h_ref[...] + _dot(mix_ref[...], wmix_ref[...])
    d = h.shape[-1]
    hd = d // XATTN_HEADS
    hn = _bf(_rms(h, nw_ref[...]))
    q = _bf(_dot(hn, wq_ref[...]) * (hd ** -0.5))
    for i in range(XATTN_HEADS):
        cs = slice(i * hd, (i + 1) * hd)
        s = _dot_nt(q[:, cs], k_ref[0, :, cs])
        s = s - jnp.max(s, axis=-1, keepdims=True)
        p = jnp.exp(s)
        p = p / jnp.sum(p, axis=-1, keepdims=True)
        att_ref[:, cs] = _bf(_dot(_bf(p), v_ref[0, :, cs]))
    o_ref[...] = h + _dot(att_ref[...], wo_ref[...])


def _xattn(mix, w_mix, h, nw, wq, kv, wo, batch, tq):
    m, d = h.shape
    t = m // batch
    mem_len = kv.shape[1]
    nt = t // tq
    row_map = lambda b, i: (b * nt + i, 0)
    const_map = lambda b, i: (0, 0)
    weight = lambda: pl.BlockSpec((d, d), const_map, pipeline_mode=pl.Buffered(1))
    return pl.pallas_call(
        _xattn_body,
        out_shape=jax.ShapeDtypeStruct((m, d), F32),
        grid=(batch, nt),
        in_specs=[
            pl.BlockSpec((tq, d), row_map),
            weight(),
            pl.BlockSpec((tq, d), row_map),
            pl.BlockSpec((1, d), const_map),
            weight(),
            pl.BlockSpec((1, mem_len, d), lambda b, i: (b, 0, 0)),
            pl.BlockSpec((1, mem_len, d), lambda b, i: (b, 0, 1)),
            weight(),
        ],
        out_specs=pl.BlockSpec((tq, d), row_map),
        scratch_shapes=[pltpu.VMEM((tq, d), BF16)],
        compiler_params=pltpu.CompilerParams(
            dimension_semantics=("parallel", "parallel"),
            vmem_limit_bytes=VMEM_LIMIT),
        name="xattn",
    )(mix, w_mix, h, nw.reshape(1, d), wq, kv, kv, wo)


def _shift_rows(x, k, tail, row):
    y = pltpu.roll(x, k, 0)
    head = y[0:8]
    for j in range(k):
        src = tail.shape[0] - k + j
        head = jnp.where(row[0:8] == j, tail[src:src + 1, :], head)
    return jnp.concatenate([head, y[8:]], axis=0)


def _conv_ffn_down_body(u_ref, halo_ref, cw_ref, wd_ref, r_ref, fw_ref, o_ref,
                        ext_ref, yst_ref, act_ref, *, tiles_per_seq, final_norm):
    tm = u_ref.shape[0]
    d_ff = act_ref.shape[1]
    first = (pl.program_id(0) % tiles_per_seq) == 0
    keep = jnp.where(first, 0.0, 1.0)
    hrows = halo_ref.shape[0]
    nq = tm // CONV_PHASES

    def phase_rows(half, off):
        return ext_ref[half, pl.ds(8 + off, nq, stride=CONV_PHASES), :]

    for c0 in range(0, d_ff, LANES):
        for half in range(2):
            cs = slice(half * d_ff + c0, half * d_ff + c0 + LANES)
            ext_ref[half, 0:8, :] = halo_ref[hrows - 8:hrows, cs].astype(F32) * keep
            ext_ref[half, 8:8 + tm, :] = u_ref[:, cs].astype(F32)
        xs = [[phase_rows(half, off) for off in range(1 - FFN_CONV, CONV_PHASES)]
              for half in range(2)]
        ws = [cw_ref[:, half * d_ff + c0:half * d_ff + c0 + LANES] for half in range(2)]
        for p in range(CONV_PHASES):
            ys = []
            for half in range(2):
                y = None
                for k in range(FFN_CONV):
                    t = xs[half][p + k] * ws[half][k:k + 1, :]
                    y = t if y is None else y + t
                ys.append(y)
            hg = ys[0]
            yst_ref[pl.ds(p, nq, stride=CONV_PHASES), :] = (hg + hg * jnp.tanh(hg)) * ys[1]
        act_ref[:, c0:c0 + LANES] = _bf(yst_ref[...])
    out = r_ref[...] + _dot(act_ref[...], wd_ref[...])
    if final_norm:
        out = _rms(out, fw_ref[...])
    o_ref[...] = out


def _conv_ffn_down(u, conv_w, w_down, res, final_w, seq_len, tm, final_norm):
    m, n2 = u.shape
    d_ff = n2 // 2
    d = w_down.shape[1]
    halo = 16
    body = functools.partial(_conv_ffn_down_body, tiles_per_seq=seq_len // tm,
                             final_norm=final_norm)
    return pl.pallas_call(
        body,
        out_shape=jax.ShapeDtypeStruct((m, d), F32),
        grid=(m // tm,),
        in_specs=[
            pl.BlockSpec((tm, n2), lambda i: (i, 0)),
            pl.BlockSpec((halo, n2), lambda i: (jnp.maximum(i * (tm // halo) - 1, 0), 0)),
            pl.BlockSpec((FFN_CONV, n2), lambda i: (0, 0)),
            pl.BlockSpec((d_ff, d), lambda i: (0, 0), pipeline_mode=pl.Buffered(1)),
            pl.BlockSpec((tm, d), lambda i: (i, 0)),
            pl.BlockSpec((1, d), lambda i: (0, 0)),
        ],
        out_specs=pl.BlockSpec((tm, d), lambda i: (i, 0)),
        scratch_shapes=[pltpu.VMEM((2, 8 + tm, LANES), F32),
                        pltpu.VMEM((tm, LANES), F32),
                        pltpu.VMEM((tm, d_ff), BF16)],
        compiler_params=pltpu.CompilerParams(
            dimension_semantics=("parallel",),
            vmem_limit_bytes=VMEM_LIMIT),
        name="conv_ffn_down",
    )(u, u, conv_w, w_down, res, final_w.reshape(1, d))


def _mixer_constants(tb):
    c = CHUNK
    ar = np.arange
    std_head = ar(GROUP) // HEAD_DIM
    rope_head = (ar(GROUP) % LANES) // ROPE_HALF
    gla_head = ar(GLA_KEYS) // GLA_KEY_DIM
    row_head = ar(GROUP) // c
    s_of = ar(GROUP) % c
    t = ar(c)
    tt = ar(tb) % c
    same_chunk = (ar(tb)[:, None] // c) == (ar(tb)[None, :] // c)

    out = {}
    bd_std = row_head[:, None] == std_head[None, :]
    out["bd_std"] = (bd_std, BF16)
    out["bd_std2"] = (np.concatenate([bd_std, bd_std], axis=1), BF16)
    out["bd_rope"] = (row_head[:, None] == rope_head[None, :], BF16)
    out["bd_gla"] = (row_head[:, None] == gla_head[None, :], BF16)

    log_gamma = np.log(1.0 - np.exp2(-5.0 - ar(N_HEADS, dtype=np.float64)))
    scale = HEAD_DIM ** -0.5
    rel = t[:, None] - s_of[None, :]
    lg_w = log_gamma[std_head][None, :]
    out["ret_d"] = (np.where(rel >= 0, np.exp(lg_w * np.maximum(rel, 0)), 0.0) * scale, F32)
    lg_r = log_gamma[rope_head][None, :]
    out["ret_qdec"] = (np.exp(lg_r * (tt[:, None] + 1.0)), F32)
    out["ret_kdec"] = (np.exp(lg_r * (c - 1.0 - tt[:, None])) * scale, F32)
    out["ret_cdec"] = (np.exp(log_gamma[std_head] * c)[None, :], F32)

    lvl, sel_q, sel_k = [], [], []
    for b in LEVELS:
        blk = t // (2 * b)
        second = (t % (2 * b)) >= b
        bound = blk * 2 * b + b - 1
        msk = (blk[:, None] == blk[None, :]) & second[:, None] & (~second)[None, :]
        lvl.append(msk[:, s_of])
        sel_q.append(second[:, None] & (t[None, :] > bound[:, None]) & (t[None, :] <= t[:, None]))
        sel_k.append((~second)[:, None] & (t[None, :] > t[:, None]) & (t[None, :] <= bound[:, None]))
    lvl.append(t[:, None] == s_of[None, :])
    out["lvl"] = (np.stack(lvl), F32)
    tri = t[None, :] <= t[:, None]
    tail = t[None, :] > t[:, None]
    eye_b = np.eye(tb // c, dtype=bool)
    sel = [np.kron(eye_b, s_) for s_ in [a | b for a, b in zip(sel_q, sel_k)] + [tri, tail]]
    out["sel"] = (np.stack(sel), BF16)
    out["ones_b"] = (same_chunk, BF16)
    out["trit_w"] = (tt[:, None] <= s_of[None, :], F32)
    out["causal_w"] = (tt[:, None] >= s_of[None, :], F32)
    out["strict_w"] = (tt[:, None] > s_of[None, :], F32)
    out["eye_w"] = (t[:, None] == s_of[None, :], F32)
    eab = np.zeros((LANES, 2 * GROUP))
    eab[SM_BA + std_head, ar(GROUP)] = 1.0
    eab[SM_BB + std_head, GROUP + ar(GROUP)] = 1.0
    out["exp_ab"] = (eab, BF16)
    return out


_CONST_ORDER = ("bd_std", "bd_std2", "bd_rope", "bd_gla", "ret_d", "ret_qdec", "ret_kdec",
                "ret_cdec", "lvl", "sel", "ones_b", "trit_w", "causal_w", "strict_w", "eye_w",
                "exp_ab")
_PARAM_ORDER = ("gdn_conv", "gdn_alog", "gdn_dt", "gdn_nw", "gla_up", "gla_bias", "gla_nw",
                "hg_loglb", "hg_log1mlb", "hg_1mlb", "hg_nw")
_SCRATCH_ORDER = ("sa", "sb", "sct", "sdt", "tail",
                  "aq", "aqd", "ak", "akd", "av",
                  "bq", "bk", "bbr", "brc", "brhs", "bqg", "bkt", "bcd",
                  "cqx", "ckx", "cv", "ccd", "dqx", "dkx", "dv", "dcd", "oscr")


def _mixer_scratch(tb):
    f, b = F32, BF16
    shapes = {
        "sa": ((GROUP, GROUP), f), "sb": ((GROUP, GROUP), f),
        "sct": ((GROUP, GLA_KEYS), f), "sdt": ((GROUP, GROUP), f),
        "tail": ((8, 3 * GROUP), f),
        "aq": ((tb, GROUP), b), "aqd": ((tb, GROUP), b), "ak": ((tb, GROUP), b),
        "akd": ((tb, GROUP), b), "av": ((tb, GROUP), b),
        "bq": ((tb, GROUP), b), "bk": ((tb, GROUP), b), "bbr": ((tb, GROUP), f),
        "brc": ((tb, GROUP), f), "brhs": ((tb, 2 * GROUP), b), "bqg": ((tb, GROUP), b),
        "bkt": ((tb, GROUP), b), "bcd": ((tb, GROUP), f),
        "cqx": ((N_LEVELS + 2, tb, GLA_KEYS), b), "ckx": ((N_LEVELS + 2, tb, GLA_KEYS), b),
        "cv": ((tb, GROUP), b), "ccd": ((tb, GLA_KEYS), f),
        "dqx": ((N_LEVELS + 2, tb, GROUP), b), "dkx": ((N_LEVELS + 2, tb, GROUP), b),
        "dv": ((tb, GROUP), b), "dcd": ((tb, GROUP), f),
        "oscr": ((tb, N_HEADS * GROUP), f),
    }
    return [pltpu.VMEM(*shapes[k]) for k in _SCRATCH_ORDER]


def _mixer_body(*refs, n_chunks):
    zb_ref, zf_ref, cos_ref, sin_ref = refs[:4]
    nc, npar = len(_CONST_ORDER), len(_PARAM_ORDER)
    cst = dict(zip(_CONST_ORDER, refs[4:4 + nc]))
    par = dict(zip(_PARAM_ORDER, refs[4 + nc:4 + nc + npar]))
    o_ref = refs[4 + nc + npar]
    scr = dict(zip(_SCRATCH_ORDER, refs[5 + nc + npar:]))
    tb = zb_ref.shape[0]
    rows = [slice(n * CHUNK, (n + 1) * CHUNK) for n in range(n_chunks)]
    bd_std = cst["bd_std"]

    def zcols(col, width):
        return zb_ref[:, col:col + width].astype(F32)

    @pl.when(pl.program_id(1) == 0)
    def _():
        for k in ("sa", "sb", "sct", "sdt", "tail"):
            scr[k][...] = jnp.zeros_like(scr[k])

    def seg_sum(x):
        return _dot(_bf(x), bd_std[...])

    cos = cos_ref[...]
    sin = sin_ref[...]

    def rope(col):
        x1 = zcols(col, LANES)
        x2 = zcols(col + LANES, LANES)
        return jnp.concatenate([x1 * cos - x2 * sin, x1 * sin + x2 * cos], axis=1)

    qr = rope(Z_RQ)
    scr["aq"][...] = _bf(qr)
    scr["aqd"][...] = _bf(qr * cst["ret_qdec"][...])
    kr = rope(Z_RK)
    scr["ak"][...] = _bf(kr)
    scr["akd"][...] = _bf(kr * cst["ret_kdec"][...])
    scr["av"][...] = zb_ref[:, Z_RV:Z_RV + GROUP]

    row = lax.broadcasted_iota(jnp.int32, (tb, 1), 0)
    x = zcols(Z_BQKV, 3 * GROUP)
    tail = scr["tail"][...]
    cw = par["gdn_conv"][...]
    y = x * cw[GDN_CONV - 1:GDN_CONV, :]
    for k in range(1, GDN_CONV):
        y = y + _shift_rows(x, k, tail, row) * cw[GDN_CONV - 1 - k:GDN_CONV - k, :]
    scr["tail"][...] = x[tb - 8:tb, :]
    qkv = _silu(y)
    qb, kb, vv = qkv[:, 0:GROUP], qkv[:, GROUP:2 * GROUP], qkv[:, 2 * GROUP:3 * GROUP]
    sm = zf_ref[:, F_SMALL:F_SMALL + LANES]
    bab = _dot_sel(sm, cst["exp_ab"][...])
    g = (-LOG2E * jnp.exp(par["gdn_alog"][...])) * _softplus(bab[:, 0:GROUP] + par["gdn_dt"][...])
    beta = jax.nn.sigmoid(bab[:, GROUP:2 * GROUP])
    qn = qb * lax.rsqrt(seg_sum(qb * qb) + EPS) * (HEAD_DIM ** -0.5)
    kn = kb * lax.rsqrt(seg_sum(kb * kb) + EPS)
    ones_b = cst["ones_b"][...]
    gc = _sel_dot(cst["sel"][N_LEVELS], g)
    gcr = _sel_dot(ones_b, g * cst["trit_w"][...])
    glast = _sel_dot(ones_b, g)
    dec = jnp.exp2(gc - gcr)
    gam = jnp.exp2(gc)
    scr["bq"][...] = _bf(qn)
    scr["bk"][...] = _bf(kn)
    scr["bbr"][...] = beta * jnp.where(cst["strict_w"][...] > 0, dec, 0.0)
    scr["brc"][...] = jnp.where(cst["causal_w"][...] > 0, dec, 0.0)
    scr["brhs"][...] = _bf(jnp.concatenate([beta * vv, beta * gam * kn], axis=1))
    scr["bqg"][...] = _bf(qn * gam)
    scr["bkt"][...] = _bf(kn * jnp.exp2(glast - gc))
    scr["bcd"][...] = jnp.exp2(glast)

    def gl_prep(q, k, v, logf, qx_ref, kx_ref, v_ref, cd_ref):
        hi, lo = _split(logf * LOG2E, 2)
        for j in range(N_LEVELS + 2):
            sel = cst["sel"][j]
            e = jnp.exp2(_dot(sel, hi) + _dot(sel, lo))
            if j < N_LEVELS:
                qx_ref[j] = _bf(q * e)
                kx_ref[j] = _bf(k * e)
            elif j == N_LEVELS:
                qx_ref[GL_EDGE] = _bf(q * e)
                cd_ref[...] = e
            else:
                kx_ref[GL_EDGE] = _bf(k * e)
            yield
        qx_ref[GL_DIAG] = _bf(q)
        kx_ref[GL_DIAG] = _bf(k)
        v_ref[...] = _bf(v)

    logit = _dot_hp(sm, par["gla_up"][...]) + par["gla_bias"][...]
    prep_c = gl_prep(zcols(Z_CQ, GLA_KEYS) * (GLA_KEY_DIM ** -0.5),
                     zcols(Z_CK, GLA_KEYS), zb_ref[:, Z_CV:Z_CV + GROUP],
                     _log_sigmoid(logit) * (1.0 / GLA_GATE_NORM),
                     scr["cqx"], scr["ckx"], scr["cv"], scr["ccd"])

    f_pre = zf_ref[:, F_DF:F_DF + GROUP]
    a = par["hg_loglb"][...]
    b = par["hg_log1mlb"][...] + _log_sigmoid(f_pre)
    logf_d = jnp.maximum(a, b) + jnp.log1p(jnp.exp(-jnp.abs(a - b)))
    prep_d = gl_prep(zcols(Z_DQ, GROUP), par["hg_1mlb"][...] * jax.nn.sigmoid(-f_pre),
                     zb_ref[:, Z_DI:Z_DI + GROUP], logf_d,
                     scr["dqx"], scr["dkx"], scr["dv"], scr["dcd"])

    pa, pb, pc, pd = {}, {}, {}, {}

    def ret_p1(n):
        rs = rows[n]
        p = _dot_nt(scr["aq"][rs, :], _blockdiag(scr["ak"][rs, :], cst["bd_rope"][...]))
        yield
        pa[n] = _bf(p * cst["ret_d"][...])

    def gl_p1(n, qx_ref, kx_ref, bd_k, store):
        rs = rows[n]
        p = None
        for l in range(N_LEVELS + 1):
            t = _dot_nt(qx_ref[l, rs, :], _blockdiag(kx_ref[l, rs, :], bd_k[...]))
            yield
            t = t * cst["lvl"][l]
            p = t if p is None else p + t
        store[n] = _bf(p)

    def wide_matmul(a, b):
        return _dot(_bf(a), _blockdiag(_bf(b), bd_std[...]))

    def gdn_p1(n):
        rs = rows[n]
        knb = scr["bk"][rs, :]
        kbd = _blockdiag(knb, bd_std[...])
        kk = _dot_nt(knb, kbd)
        qk = _dot_nt(scr["bq"][rs, :], kbd)
        yield
        lmat = scr["bbr"][rs, :] * kk
        tinv = cst["eye_w"][...] - lmat * cst["lvl"][0]
        for li in range(1, N_LEVELS):
            xb = wide_matmul(lmat * cst["lvl"][li], tinv)
            yield
            tinv = tinv - wide_matmul(tinv, xb)
            yield
        uw = _dot(_bf(tinv), _blockdiag(scr["brhs"][rs, :], cst["bd_std2"][...]))
        yield
        pb[n] = (uw[:, 0:GROUP], _bf(uw[:, GROUP:2 * GROUP]), _bf(qk * scr["brc"][rs, :]))

    oscr = scr["oscr"]

    def ret_p2():
        s = scr["sa"][...]
        for n, rs in enumerate(rows):
            vbd = _blockdiag(scr["av"][rs, :], bd_std[...])
            o = _dot(pa[n], vbd) + _dot(scr["aqd"][rs, :], _bf(s))
            kv = _dot_tn(_blockdiag(scr["akd"][rs, :], cst["bd_rope"][...]), vbd)
            yield
            s = s * cst["ret_cdec"][...] + kv
            oscr[rs, 0:GROUP] = o
        scr["sa"][...] = s

    def gdn_p2():
        s = scr["sb"][...]
        for n, rs in enumerate(rows):
            u_v, w_k, qk = pb[n]
            r = _dot(jnp.concatenate([w_k, scr["bqg"][rs, :]], axis=0), _bf(s))
            yield
            u = u_v - r[0:CHUNK]
            ubd = _blockdiag(_bf(u), bd_std[...])
            o = r[CHUNK:2 * CHUNK] + _dot(qk, ubd)
            kv = _dot_tn(_blockdiag(scr["bkt"][rs, :], bd_std[...]), ubd)
            yield
            s = s * scr["bcd"][rs.start:rs.start + 1, :] + kv
            oscr[rs, GROUP:2 * GROUP] = o
        scr["sb"][...] = s

    def gl_p2(store, qx_ref, kx_ref, v_ref, cd_ref, bd_k, st_ref, col):
        st = st_ref[...]
        for n, rs in enumerate(rows):
            vbd = _blockdiag(v_ref[rs, :], bd_std[...])
            o = _dot(store[n], vbd) + _dot_nt(qx_ref[GL_EDGE, rs, :], _bf(st))
            kv = _dot_tn(vbd, _blockdiag(kx_ref[GL_EDGE, rs, :], bd_k[...]))
            yield
            st = st * cd_ref[rs.stop - 1:rs.stop, :] + kv
            oscr[rs, col:col + GROUP] = o
        st_ref[...] = st

    def finish(idx):
        inv_hd = 1.0 / HEAD_DIM
        gains = (None, par["gdn_nw"], par["gla_nw"], par["hg_nw"])
        gates = (Z_RG, Z_BG, Z_CG, Z_DG)
        for i in idx:
            o = oscr[:, i * GROUP:(i + 1) * GROUP]
            o = o * lax.rsqrt(seg_sum(o * o) * inv_hd + EPS)
            yield
            if gains[i] is not None:
                o = o * gains[i][...]
            o_ref[:, i * GROUP:(i + 1) * GROUP] = _bf(o * _silu(zcols(gates[i], GROUP)))

    _run_interleaved([g for n in range(n_chunks) for g in (gdn_p1(n), ret_p1(n))]
                     + [prep_c, prep_d])
    _run_interleaved([g for n in range(n_chunks)
                      for g in (gl_p1(n, scr["cqx"], scr["ckx"], cst["bd_gla"], pc),
                                gl_p1(n, scr["dqx"], scr["dkx"], bd_std, pd))]
                     + [gdn_p2(), ret_p2()])
    _run_interleaved([
        gl_p2(pc, scr["cqx"], scr["ckx"], scr["cv"], scr["ccd"], cst["bd_gla"], scr["sct"],
              2 * GROUP),
        gl_p2(pd, scr["dqx"], scr["dkx"], scr["dv"], scr["dcd"], bd_std, scr["sdt"],
              3 * GROUP),
        finish((0, 1))])
    _run_interleaved([finish((2, 3))])


def _mixer(zb, zf, cos_t, sin_t, params, batch, tb):
    m = zb.shape[0]
    t = m // batch
    nt = t // tb
    consts = {k: jnp.asarray(v, dt) for k, (v, dt) in _mixer_constants(tb).items()}
    const_args = [consts[k] for k in _CONST_ORDER]
    param_args = [params[k] for k in _PARAM_ORDER]

    def full_spec(a):
        nd = a.ndim
        return pl.BlockSpec(a.shape, lambda b, i, _nd=nd: (0,) * _nd)

    row_map = lambda b, i: (b * nt + i, 0)
    body = functools.partial(_mixer_body, n_chunks=tb // CHUNK)
    return pl.pallas_call(
        body,
        out_shape=jax.ShapeDtypeStruct((m, N_HEADS * GROUP), BF16),
        grid=(batch, nt),
        in_specs=[pl.BlockSpec((tb, Z_WIDTH), row_map),
                  pl.BlockSpec((tb, F_WIDTH), row_map),
                  pl.BlockSpec((tb, LANES), row_map),
                  pl.BlockSpec((tb, LANES), row_map)]
                 + [full_spec(a) for a in const_args]
                 + [full_spec(a) for a in param_args],
        out_specs=pl.BlockSpec((tb, N_HEADS * GROUP), row_map),
        scratch_shapes=_mixer_scratch(tb),
        compiler_params=pltpu.CompilerParams(
            dimension_semantics=("parallel", "arbitrary"),
            vmem_limit_bytes=VMEM_LIMIT),
        name="mixer",
    )(zb, zf, cos_t, sin_t, *const_args, *param_args)


def _in_proj_segments():
    src = {}
    off = 0
    for name, width in (("rq", 256), ("rk", 256), ("rv", 256), ("rg", 256),
                        ("bq", 256), ("bk", 256), ("bv", 256), ("ba", 4), ("bb", 4), ("bg", 256),
                        ("cq", 128), ("ck", 128), ("cv", 256), ("clr", 16), ("cg", 256),
                        ("dq", 256), ("df", 256), ("di", 256), ("dg", 256)):
        src[name] = off
        off += width

    def rope(name):
        return [(src[name] + h * HEAD_DIM + half * ROPE_HALF, ROPE_HALF)
                for half in range(2) for h in range(N_HEADS)]

    segs = rope("rq") + rope("rk")
    segs += [(src["rv"], 256), (src["rg"], 256), (src["bq"], 768), (src["bg"], 256),
             (src["cq"], 128), (src["ck"], 128), (src["cv"], 256), (src["cg"], 256),
             (src["dq"], 256), (src["di"], 256), (src["dg"], 256)]
    assert sum(w for _, w in segs) == Z_WIDTH
    segs += [(src["df"], 256), (src["ba"], 8), (src["clr"], GLA_LOWRANK),
             (None, LANES - SM_LR - GLA_LOWRANK)]
    assert sum(w for _, w in segs) == Z_WIDTH + F_WIDTH
    return segs


def kernel(x, mem, positions, mix_norm_w, w_in, gdn_conv_w, gdn_a_log, gdn_dt_bias, gdn_norm_w,
           gla_gk_up, gla_gk_bias, gla_norm_w, hgrn_lb_logits, hgrn_norm_w, w_out,
           xattn_norm_w, mem_norm_w, xattn_wq, xattn_wk, xattn_wv, xattn_wo,
           ffn_norm_w, ffn_up, ffn_conv_w, ffn_down, final_norm_w):
    batch, seq, d = x.shape
    depth = w_in.shape[0]
    m = batch * seq
    mem_len = mem.shape[1]

    w_in_p = jnp.concatenate(
        [jnp.zeros((depth, d, wd), BF16) if s0 is None else w_in[:, :, s0:s0 + wd].astype(BF16)
         for s0, wd in _in_proj_segments()], axis=2)

    freqs = ROPE_BASE ** (-jnp.arange(0, HEAD_DIM, 2, dtype=F32) / HEAD_DIM)
    freq_row = jnp.tile(freqs, N_HEADS).reshape(1, LANES)
    cos_t, sin_t = _rope_tables(positions.reshape(m, 1), freq_row, 1024)

    lb_all = jnp.cumsum(jax.nn.softmax(hgrn_lb_logits.astype(F32), axis=0), axis=0)
    lb_all = lb_all - lb_all[0]

    def head_row(v):
        return jnp.tile(v.astype(F32), N_HEADS).reshape(1, GROUP)

    h = x.reshape(m, d)
    mem2 = mem.reshape(batch * mem_len, d)
    for l in range(depth):
        gla_up = jnp.zeros((LANES, GLA_KEYS), F32).at[SM_LR:SM_LR + GLA_LOWRANK].set(gla_gk_up[l])
        params = {
            "gdn_conv": gdn_conv_w[l].astype(F32),
            "gdn_alog": jnp.repeat(gdn_a_log[l].astype(F32), HEAD_DIM).reshape(1, GROUP),
            "gdn_dt": jnp.repeat(gdn_dt_bias[l].astype(F32), HEAD_DIM).reshape(1, GROUP),
            "gdn_nw": head_row(gdn_norm_w[l]),
            "gla_up": gla_up,
            "gla_bias": gla_gk_bias[l].astype(F32).reshape(1, GLA_KEYS),
            "gla_nw": head_row(gla_norm_w[l]),
            "hg_loglb": jnp.log(lb_all[l]).reshape(1, GROUP),
            "hg_log1mlb": jnp.log1p(-lb_all[l]).reshape(1, GROUP),
            "hg_1mlb": (1.0 - lb_all[l]).reshape(1, GROUP),
            "hg_nw": head_row(hgrn_norm_w[l]),
        }
        zb, zf = _norm_matmul(h, mix_norm_w[l], w_in_p[l],
                              ((Z_WIDTH, BF16), (F_WIDTH, F32)), TM_IN_PROJ)
        o = _mixer(zb, zf, cos_t, sin_t, params, batch, TB_MIXER)

        w_kv = jnp.concatenate([xattn_wk[l], xattn_wv[l]], axis=1).astype(BF16)
        kv, = _norm_matmul(mem2, mem_norm_w, w_kv, ((2 * d, BF16),), TM_KV)
        h = _xattn(o, w_out[l].astype(BF16), h, xattn_norm_w[l], xattn_wq[l].astype(BF16),
                   kv.reshape(batch, mem_len, 2 * d), xattn_wo[l].astype(BF16), batch, TQ_XATTN)

        u, = _norm_matmul(h, ffn_norm_w[l], ffn_up[l].astype(BF16), ((ffn_up.shape[2], BF16),),
                          TM_FFN_UP)
        d_ff = ffn_down.shape[1]
        conv_w = ffn_conv_w[l].astype(F32) * jnp.where(jnp.arange(2 * d_ff) < d_ff, 0.5, 1.0)
        h = _conv_ffn_down(u, conv_w, ffn_down[l].astype(BF16), h,
                           final_norm_w, seq, TM_FFN_DOWN, l == depth - 1)
    return h.reshape(batch, seq, d)
```

```python
import functools

import numpy as np
import jax
import jax.numpy as jnp
from jax import lax
from jax.experimental import pallas as pl
from jax.experimental.pallas import tpu as pltpu

F32 = jnp.float32
BF16 = jnp.bfloat16

N_HEADS = 4
HEAD_DIM = 64
GROUP = N_HEADS * HEAD_DIM
GLA_KEY_DIM = 32
GLA_KEYS = N_HEADS * GLA_KEY_DIM
GLA_LOWRANK = 16
GLA_GATE_NORM = 16.0
GDN_CONV = 4
CHUNK = 64
ROPE_BASE = 10000.0
ROPE_HALF = HEAD_DIM // 2
XATTN_HEADS = 4
FFN_CONV = 3
CONV_PHASES = 4
EPS = 1e-6
LOG2E = 1.4426950408889634

LANES = 128
VMEM_LIMIT = 52 * 1024 * 1024

TM_IN_PROJ = 1024
TM_KV = 1024
TQ_XATTN = 1024
TM_FFN_UP = 1024
TM_FFN_DOWN = 512
TB_MIXER = 256

Z_RQ, Z_RK, Z_RV, Z_RG = 0, 256, 512, 768
Z_BQKV, Z_BG = 1024, 1792
Z_CQ, Z_CK, Z_CV, Z_CG = 2048, 2176, 2304, 2560
Z_DQ, Z_DI, Z_DG = 2816, 3072, 3328
Z_WIDTH = 3584
F_DF, F_SMALL = 0, 256
F_WIDTH = 384
SM_BA, SM_BB, SM_LR = 0, 4, 8

LEVELS = (1, 2, 4, 8, 16, 32)
N_LEVELS = len(LEVELS)
GL_DIAG = N_LEVELS
GL_EDGE = N_LEVELS + 1


def _bf(x):
    return x.astype(BF16)


def _dot(a, b):
    return jnp.dot(a, b, preferred_element_type=F32)


def _dot_nt(a, b):
    return lax.dot_general(a, b, (((1,), (1,)), ((), ())), preferred_element_type=F32)


def _dot_tn(a, b):
    return lax.dot_general(a, b, (((0,), (0,)), ((), ())), preferred_element_type=F32)


def _split(x, n):
    parts = []
    r = x
    for i in range(n):
        p = r.astype(BF16)
        parts.append(p)
        if i < n - 1:
            r = r - p.astype(F32)
    return parts


def _sel_dot(c, x, n=2):
    acc = None
    for p in _split(x, n):
        t = _dot(c, p)
        acc = t if acc is None else acc + t
    return acc


def _dot_sel(x, c, n=2):
    acc = None
    for p in _split(x, n):
        t = _dot(p, c)
        acc = t if acc is None else acc + t
    return acc


def _dot_hp(a, b):
    a_hi, a_lo = _split(a, 2)
    b_hi, b_lo = _split(b, 2)
    return _dot(a_hi, b_hi) + _dot(a_lo, b_hi) + _dot(a_hi, b_lo)


def _silu(x):
    return x * jax.nn.sigmoid(x)


def _log_sigmoid(x):
    return jnp.minimum(x, 0.0) - jnp.log1p(jnp.exp(-jnp.abs(x)))


def _softplus(x):
    return jnp.maximum(x, 0.0) + jnp.log1p(jnp.exp(-jnp.abs(x)))


def _rms(x, w):
    ms = jnp.mean(x * x, axis=-1, keepdims=True)
    return x * lax.rsqrt(ms + EPS) * w


def _blockdiag(x, mask):
    return jnp.concatenate([x, x, x, x], axis=0) * mask


def _run_interleaved(gens):
    gens = list(gens)
    while gens:
        alive = []
        for g in gens:
            try:
                next(g)
                alive.append(g)
            except StopIteration:
                pass
        gens = alive


def _col_chunks(n, width):
    return [(c0, min(width, n - c0)) for c0 in range(0, n, width)]


def _norm_matmul_body(x_ref, nw_ref, w_ref, *o_refs, widths, col_chunk):
    xn = _bf(_rms(x_ref[...], nw_ref[...]))
    base = 0
    for o_ref, width in zip(o_refs, widths):
        for c0, cw in _col_chunks(width, col_chunk):
            o_ref[:, c0:c0 + cw] = _dot(xn, w_ref[:, base + c0:base + c0 + cw]).astype(o_ref.dtype)
        base += width


def _norm_matmul(x, nw, w, outs, tm):
    m, d = x.shape
    n = w.shape[1]
    widths = tuple(wd for wd, _ in outs)
    assert sum(widths) == n
    body = functools.partial(_norm_matmul_body, widths=widths, col_chunk=1024)
    res = pl.pallas_call(
        body,
        out_shape=tuple(jax.ShapeDtypeStruct((m, wd), dt) for wd, dt in outs),
        grid=(m // tm,),
        in_specs=[
            pl.BlockSpec((tm, d), lambda i: (i, 0)),
            pl.BlockSpec((1, d), lambda i: (0, 0)),
            pl.BlockSpec((d, n), lambda i: (0, 0), pipeline_mode=pl.Buffered(1)),
        ],
        out_specs=tuple(pl.BlockSpec((tm, wd), lambda i: (i, 0)) for wd, _ in outs),
        compiler_params=pltpu.CompilerParams(
            dimension_semantics=("parallel",),
            vmem_limit_bytes=VMEM_LIMIT),
        name="norm_matmul",
    )(x, nw.reshape(1, d), w)
    return res


def _rope_table_body(pos_ref, freq_ref, cos_ref, sin_ref):
    ang = pos_ref[...].astype(F32) * freq_ref[...]
    cos_ref[...] = jnp.cos(ang)
    sin_ref[...] = jnp.sin(ang)


def _rope_tables(pos, freq_row, tm):
    m = pos.shape[0]
    return pl.pallas_call(
        _rope_table_body,
        out_shape=(jax.ShapeDtypeStruct((m, LANES), F32),
                   jax.ShapeDtypeStruct((m, LANES), F32)),
        grid=(m // tm,),
        in_specs=[
            pl.BlockSpec((tm, 1), lambda i: (i, 0)),
            pl.BlockSpec((1, LANES), lambda i: (0, 0)),
        ],
        out_specs=(pl.BlockSpec((tm, LANES), lambda i: (i, 0)),
                   pl.BlockSpec((tm, LANES), lambda i: (i, 0))),
        compiler_params=pltpu.CompilerParams(dimension_semantics=("parallel",)),
        name="rope_tables",
    )(pos, freq_row)


def _xattn_body(mix_ref, wmix_ref, h_ref, nw_ref, wq_ref, k_ref, v_ref, wo_ref, o_ref, att_ref):
    h = h_ref[...] + _dot(mix_ref[...], wmix_ref[...])
    d = h.shape[-1]
    hd = d // XATTN_HEADS
    hn = _bf(_rms(h, nw_ref[...]))
    q = _bf(_dot(hn, wq_ref[...]) * (hd ** -0.5))
    for i in range(XATTN_HEADS):
        cs = slice(i * hd, (i + 1) * hd)
        s = _dot_nt(q[:, cs], k_ref[0, :, cs])
        s = s - jnp.max(s, axis=-1, keepdims=True)
        p = jnp.exp(s)
        p = p / jnp.sum(p, axis=-1, keepdims=True)
        att_ref[:, cs] = _bf(_dot(_bf(p), v_ref[0, :, cs]))
    o_ref[...] = h + _dot(att_ref[...], wo_ref[...])


def _xattn(mix, w_mix, h, nw, wq, kv, wo, batch, tq):
    m, d = h.shape
    t = m // batch
    mem_len = kv.shape[1]
    nt = t // tq
    row_map = lambda b, i: (b * nt + i, 0)
    const_map = lambda b, i: (0, 0)
    weight = lambda: pl.BlockSpec((d, d), const_map, pipeline_mode=pl.Buffered(1))
    return pl.pallas_call(
        _xattn_body,
        out_shape=jax.ShapeDtypeStruct((m, d), F32),
        grid=(batch, nt),
        in_specs=[
            pl.BlockSpec((tq, d), row_map),
            weight(),
            pl.BlockSpec((tq, d), row_map),
            pl.BlockSpec((1, d), const_map),
            weight(),
            pl.BlockSpec((1, mem_len, d), lambda b, i: (b, 0, 0)),
            pl.BlockSpec((1, mem_len, d), lambda b, i: (b, 0, 1)),
            weight(),
        ],
        out_specs=pl.BlockSpec((tq, d), row_map),
        scratch_shapes=[pltpu.VMEM((tq, d), BF16)],
        compiler_params=pltpu.CompilerParams(
            dimension_semantics=("parallel", "parallel"),
            vmem_limit_bytes=VMEM_LIMIT),
        name="xattn",
    )(mix, w_mix, h, nw.reshape(1, d), wq, kv, kv, wo)


def _shift_rows(x, k, tail, row):
    y = pltpu.roll(x, k, 0)
    head = y[0:8]
    for j in range(k):
        src = tail.shape[0] - k + j
        head = jnp.where(row[0:8] == j, tail[src:src + 1, :], head)
    return jnp.concatenate([head, y[8:]], axis=0)


def _conv_ffn_down_body(u_ref, halo_ref, cw_ref, wd_ref, r_ref, fw_ref, o_ref,
                        ext_ref, yst_ref, act_ref, *, tiles_per_seq, final_norm):
    tm = u_ref.shape[0]
    d_ff = act_ref.shape[1]
    first = (pl.program_id(0) % tiles_per_seq) == 0
    keep = jnp.where(first, 0.0, 1.0)
    hrows = halo_ref.shape[0]
    nq = tm // CONV_PHASES

    def phase_rows(half, off):
        return ext_ref[half, pl.ds(8 + off, nq, stride=CONV_PHASES), :]

    for c0 in range(0, d_ff, LANES):
        for half in range(2):
            cs = slice(half * d_ff + c0, half * d_ff + c0 + LANES)
            ext_ref[half, 0:8, :] = halo_ref[hrows - 8:hrows, cs].astype(F32) * keep
            ext_ref[half, 8:8 + tm, :] = u_ref[:, cs].astype(F32)
        xs = [[phase_rows(half, off) for off in range(1 - FFN_CONV, CONV_PHASES)]
              for half in range(2)]
        ws = [cw_ref[:, half * d_ff + c0:half * d_ff + c0 + LANES] for half in range(2)]
        for p in range(CONV_PHASES):
            ys = []
            for half in range(2):
                y = None
                for k in range(FFN_CONV):
                    t = xs[half][p + k] * ws[half][k:k + 1, :]
                    y = t if y is None else y + t
                ys.append(y)
            hg = ys[0]
            yst_ref[pl.ds(p, nq, stride=CONV_PHASES), :] = (hg + hg * jnp.tanh(hg)) * ys[1]
        act_ref[:, c0:c0 + LANES] = _bf(yst_ref[...])
    out = r_ref[...] + _dot(act_ref[...], wd_ref[...])
    if final_norm:
        out = _rms(out, fw_ref[...])
    o_ref[...] = out


def _conv_ffn_down(u, conv_w, w_down, res, final_w, seq_len, tm, final_norm):
    m, n2 = u.shape
    d_ff = n2 // 2
    d = w_down.shape[1]
    halo = 16
    body = functools.partial(_conv_ffn_down_body, tiles_per_seq=seq_len // tm,
                             final_norm=final_norm)
    return pl.pallas_call(
        body,
        out_shape=jax.ShapeDtypeStruct((m, d), F32),
        grid=(m // tm,),
        in_specs=[
            pl.BlockSpec((tm, n2), lambda i: (i, 0)),
            pl.BlockSpec((halo, n2), lambda i: (jnp.maximum(i * (tm // halo) - 1, 0), 0)),
            pl.BlockSpec((FFN_CONV, n2), lambda i: (0, 0)),
            pl.BlockSpec((d_ff, d), lambda i: (0, 0), pipeline_mode=pl.Buffered(1)),
            pl.BlockSpec((tm, d), lambda i: (i, 0)),
            pl.BlockSpec((1, d), lambda i: (0, 0)),
        ],
        out_specs=pl.BlockSpec((tm, d), lambda i: (i, 0)),
        scratch_shapes=[pltpu.VMEM((2, 8 + tm, LANES), F32),
                        pltpu.VMEM((tm, LANES), F32),
                        pltpu.VMEM((tm, d_ff), BF16)],
        compiler_params=pltpu.CompilerParams(
            dimension_semantics=("parallel",),
            vmem_limit_bytes=VMEM_LIMIT),
        name="conv_ffn_down",
    )(u, u, conv_w, w_down, res, final_w.reshape(1, d))


def _mixer_constants(tb):
    c = CHUNK
    ar = np.arange
    std_head = ar(GROUP) // HEAD_DIM
    rope_head = (ar(GROUP) % LANES) // ROPE_HALF
    gla_head = ar(GLA_KEYS) // GLA_KEY_DIM
    row_head = ar(GROUP) // c
    s_of = ar(GROUP) % c
    t = ar(c)
    tt = ar(tb) % c
    same_chunk = (ar(tb)[:, None] // c) == (ar(tb)[None, :] // c)

    out = {}
    bd_std = row_head[:, None] == std_head[None, :]
    out["bd_std"] = (bd_std, BF16)
    out["bd_std2"] = (np.concatenate([bd_std, bd_std], axis=1), BF16)
    out["bd_rope"] = (row_head[:, None] == rope_head[None, :], BF16)
    out["bd_gla"] = (row_head[:, None] == gla_head[None, :], BF16)

    log_gamma = np.log(1.0 - np.exp2(-5.0 - ar(N_HEADS, dtype=np.float64)))
    scale = HEAD_DIM ** -0.5
    rel = t[:, None] - s_of[None, :]
    lg_w = log_gamma[std_head][None, :]
    out["ret_d"] = (np.where(rel >= 0, np.exp(lg_w * np.maximum(rel, 0)), 0.0) * scale, F32)
    lg_r = log_gamma[rope_head][None, :]
    out["ret_qdec"] = (np.exp(lg_r * (tt[:, None] + 1.0)), F32)
    out["ret_kdec"] = (np.exp(lg_r * (c - 1.0 - tt[:, None])) * scale, F32)
    out["ret_cdec"] = (np.exp(log_gamma[std_head] * c)[None, :], F32)

    lvl, sel_q, sel_k = [], [], []
    for b in LEVELS:
        blk = t // (2 * b)
        second = (t % (2 * b)) >= b
        bound = blk * 2 * b + b - 1
        msk = (blk[:, None] == blk[None, :]) & second[:, None] & (~second)[None, :]
        lvl.append(msk[:, s_of])
        sel_q.append(second[:, None] & (t[None, :] > bound[:, None]) & (t[None, :] <= t[:, None]))
        sel_k.append((~second)[:, None] & (t[None, :] > t[:, None]) & (t[None, :] <= bound[:, None]))
    lvl.append(t[:, None] == s_of[None, :])
    out["lvl"] = (np.stack(lvl), F32)
    tri = t[None, :] <= t[:, None]
    tail = t[None, :] > t[:, None]
    eye_b = np.eye(tb // c, dtype=bool)
    sel = [np.kron(eye_b, s_) for s_ in [a | b for a, b in zip(sel_q, sel_k)] + [tri, tail]]
    out["sel"] = (np.stack(sel), BF16)
    out["ones_b"] = (same_chunk, BF16)
    out["trit_w"] = (tt[:, None] <= s_of[None, :], F32)
    out["causal_w"] = (tt[:, None] >= s_of[None, :], F32)
    out["strict_w"] = (tt[:, None] > s_of[None, :], F32)
    out["eye_w"] = (t[:, None] == s_of[None, :], F32)
    eab = np.zeros((LANES, 2 * GROUP))
    eab[SM_BA + std_head, ar(GROUP)] = 1.0
    eab[SM_BB + std_head, GROUP + ar(GROUP)] = 1.0
    out["exp_ab"] = (eab, BF16)
    return out


_CONST_ORDER = ("bd_std", "bd_std2", "bd_rope", "bd_gla", "ret_d", "ret_qdec", "ret_kdec",
                "ret_cdec", "lvl", "sel", "ones_b", "trit_w", "causal_w", "strict_w", "eye_w",
                "exp_ab")
_PARAM_ORDER = ("gdn_conv", "gdn_alog", "gdn_dt", "gdn_nw", "gla_up", "gla_bias", "gla_nw",
                "hg_loglb", "hg_log1mlb", "hg_1mlb", "hg_nw")
_SCRATCH_ORDER = ("sa", "sb", "sct", "sdt", "tail",
                  "aq", "aqd", "ak", "akd", "av",
                  "bq", "bk", "bbr", "brc", "brhs", "bqg", "bkt", "bcd",
                  "cqx", "ckx", "cv", "ccd", "dqx", "dkx", "dv", "dcd", "oscr")


def _mixer_scratch(tb):
    f, b = F32, BF16
    shapes = {
        "sa": ((GROUP, GROUP), f), "sb": ((GROUP, GROUP), f),
        "sct": ((GROUP, GLA_KEYS), f), "sdt": ((GROUP, GROUP), f),
        "tail": ((8, 3 * GROUP), f),
        "aq": ((tb, GROUP), b), "aqd": ((tb, GROUP), b), "ak": ((tb, GROUP), b),
        "akd": ((tb, GROUP), b), "av": ((tb, GROUP), b),
        "bq": ((tb, GROUP), b), "bk": ((tb, GROUP), b), "bbr": ((tb, GROUP), f),
        "brc": ((tb, GROUP), f), "brhs": ((tb, 2 * GROUP), b), "bqg": ((tb, GROUP), b),
        "bkt": ((tb, GROUP), b), "bcd": ((tb, GROUP), f),
        "cqx": ((N_LEVELS + 2, tb, GLA_KEYS), b), "ckx": ((N_LEVELS + 2, tb, GLA_KEYS), b),
        "cv": ((tb, GROUP), b), "ccd": ((tb, GLA_KEYS), f),
        "dqx": ((N_LEVELS + 2, tb, GROUP), b), "dkx": ((N_LEVELS + 2, tb, GROUP), b),
        "dv": ((tb, GROUP), b), "dcd": ((tb, GROUP), f),
        "oscr": ((tb, N_HEADS * GROUP), f),
    }
    return [pltpu.VMEM(*shapes[k]) for k in _SCRATCH_ORDER]


def _mixer_body(*refs, n_chunks):
    zb_ref, zf_ref, cos_ref, sin_ref = refs[:4]
    nc, npar = len(_CONST_ORDER), len(_PARAM_ORDER)
    cst = dict(zip(_CONST_ORDER, refs[4:4 + nc]))
    par = dict(zip(_PARAM_ORDER, refs[4 + nc:4 + nc + npar]))
    o_ref = refs[4 + nc + npar]
    scr = dict(zip(_SCRATCH_ORDER, refs[5 + nc + npar:]))
    tb = zb_ref.shape[0]
    rows = [slice(n * CHUNK, (n + 1) * CHUNK) for n in range(n_chunks)]
    bd_std = cst["bd_std"]

    def zcols(col, width):
        return zb_ref[:, col:col + width].astype(F32)

    @pl.when(pl.program_id(1) == 0)
    def _():
        for k in ("sa", "sb", "sct", "sdt", "tail"):
            scr[k][...] = jnp.zeros_like(scr[k])

    def seg_sum(x):
        return _dot(_bf(x), bd_std[...])

    cos = cos_ref[...]
    sin = sin_ref[...]

    def rope(col):
        x1 = zcols(col, LANES)
        x2 = zcols(col + LANES, LANES)
        return jnp.concatenate([x1 * cos - x2 * sin, x1 * sin + x2 * cos], axis=1)

    qr = rope(Z_RQ)
    scr["aq"][...] = _bf(qr)
    scr["aqd"][...] = _bf(qr * cst["ret_qdec"][...])
    kr = rope(Z_RK)
    scr["ak"][...] = _bf(kr)
    scr["akd"][...] = _bf(kr * cst["ret_kdec"][...])
    scr["av"][...] = zb_ref[:, Z_RV:Z_RV + GROUP]

    row = lax.broadcasted_iota(jnp.int32, (tb, 1), 0)
    x = zcols(Z_BQKV, 3 * GROUP)
    tail = scr["tail"][...]
    cw = par["gdn_conv"][...]
    y = x * cw[GDN_CONV - 1:GDN_CONV, :]
    for k in range(1, GDN_CONV):
        y = y + _shift_rows(x, k, tail, row) * cw[GDN_CONV - 1 - k:GDN_CONV - k, :]
    scr["tail"][...] = x[tb - 8:tb, :]
    qkv = _silu(y)
    qb, kb, vv = qkv[:, 0:GROUP], qkv[:, GROUP:2 * GROUP], qkv[:, 2 * GROUP:3 * GROUP]
    sm = zf_ref[:, F_SMALL:F_SMALL + LANES]
    bab = _dot_sel(sm, cst["exp_ab"][...])
    g = (-LOG2E * jnp.exp(par["gdn_alog"][...])) * _softplus(bab[:, 0:GROUP] + par["gdn_dt"][...])
    beta = jax.nn.sigmoid(bab[:, GROUP:2 * GROUP])
    qn = qb * lax.rsqrt(seg_sum(qb * qb) + EPS) * (HEAD_DIM ** -0.5)
    kn = kb * lax.rsqrt(seg_sum(kb * kb) + EPS)
    ones_b = cst["ones_b"][...]
    gc = _sel_dot(cst["sel"][N_LEVELS], g)
    gcr = _sel_dot(ones_b, g * cst["trit_w"][...])
    glast = _sel_dot(ones_b, g)
    dec = jnp.exp2(gc - gcr)
    gam = jnp.exp2(gc)
    scr["bq"][...] = _bf(qn)
    scr["bk"][...] = _bf(kn)
    scr["bbr"][...] = beta * jnp.where(cst["strict_w"][...] > 0, dec, 0.0)
    scr["brc"][...] = jnp.where(cst["causal_w"][...] > 0, dec, 0.0)
    scr["brhs"][...] = _bf(jnp.concatenate([beta * vv, beta * gam * kn], axis=1))
    scr["bqg"][...] = _bf(qn * gam)
    scr["bkt"][...] = _bf(kn * jnp.exp2(glast - gc))
    scr["bcd"][...] = jnp.exp2(glast)

    def gl_prep(q, k, v, logf, qx_ref, kx_ref, v_ref, cd_ref):
        hi, lo = _split(logf * LOG2E, 2)
        for j in range(N_LEVELS + 2):
            sel = cst["sel"][j]
            e = jnp.exp2(_dot(sel, hi) + _dot(sel, lo))
            if j < N_LEVELS:
                qx_ref[j] = _bf(q * e)
                kx_ref[j] = _bf(k * e)
            elif j == N_LEVELS:
                qx_ref[GL_EDGE] = _bf(q * e)
                cd_ref[...] = e
            else:
                kx_ref[GL_EDGE] = _bf(k * e)
            yield
        qx_ref[GL_DIAG] = _bf(q)
        kx_ref[GL_DIAG] = _bf(k)
        v_ref[...] = _bf(v)

    logit = _dot_hp(sm, par["gla_up"][...]) + par["gla_bias"][...]
    prep_c = gl_prep(zcols(Z_CQ, GLA_KEYS) * (GLA_KEY_DIM ** -0.5),
                     zcols(Z_CK, GLA_KEYS), zb_ref[:, Z_CV:Z_CV + GROUP],
                     _log_sigmoid(logit) * (1.0 / GLA_GATE_NORM),
                     scr["cqx"], scr["ckx"], scr["cv"], scr["ccd"])

    f_pre = zf_ref[:, F_DF:F_DF + GROUP]
    a = par["hg_loglb"][...]
    b = par["hg_log1mlb"][...] + _log_sigmoid(f_pre)
    logf_d = jnp.maximum(a, b) + jnp.log1p(jnp.exp(-jnp.abs(a - b)))
    prep_d = gl_prep(zcols(Z_DQ, GROUP), par["hg_1mlb"][...] * jax.nn.sigmoid(-f_pre),
                     zb_ref[:, Z_DI:Z_DI + GROUP], logf_d,
                     scr["dqx"], scr["dkx"], scr["dv"], scr["dcd"])

    pa, pb, pc, pd = {}, {}, {}, {}

    def ret_p1(n):
        rs = rows[n]
        p = _dot_nt(scr["aq"][rs, :], _blockdiag(scr["ak"][rs, :], cst["bd_rope"][...]))
        yield
        pa[n] = _bf(p * cst["ret_d"][...])

    def gl_p1(n, qx_ref, kx_ref, bd_k, store):
        rs = rows[n]
        p = None
        for l in range(N_LEVELS + 1):
            t = _dot_nt(qx_ref[l, rs, :], _blockdiag(kx_ref[l, rs, :], bd_k[...]))
            yield
            t = t * cst["lvl"][l]
            p = t if p is None else p + t
        store[n] = _bf(p)

    def wide_matmul(a, b):
        return _dot(_bf(a), _blockdiag(_bf(b), bd_std[...]))

    def gdn_p1(n):
        rs = rows[n]
        knb = scr["bk"][rs, :]
        kbd = _blockdiag(knb, bd_std[...])
        kk = _dot_nt(knb, kbd)
        qk = _dot_nt(scr["bq"][rs, :], kbd)
        yield
        lmat = scr["bbr"][rs, :] * kk
        tinv = cst["eye_w"][...] - lmat * cst["lvl"][0]
        for li in range(1, N_LEVELS):
            xb = wide_matmul(lmat * cst["lvl"][li], tinv)
            yield
            tinv = tinv - wide_matmul(tinv, xb)
            yield
        uw = _dot(_bf(tinv), _blockdiag(scr["brhs"][rs, :], cst["bd_std2"][...]))
        yield
        pb[n] = (uw[:, 0:GROUP], _bf(uw[:, GROUP:2 * GROUP]), _bf(qk * scr["brc"][rs, :]))

    oscr = scr["oscr"]

    def ret_p2():
        s = scr["sa"][...]
        for n, rs in enumerate(rows):
            vbd = _blockdiag(scr["av"][rs, :], bd_std[...])
            o = _dot(pa[n], vbd) + _dot(scr["aqd"][rs, :], _bf(s))
            kv = _dot_tn(_blockdiag(scr["akd"][rs, :], cst["bd_rope"][...]), vbd)
            yield
            s = s * cst["ret_cdec"][...] + kv
            oscr[rs, 0:GROUP] = o
        scr["sa"][...] = s

    def gdn_p2():
        s = scr["sb"][...]
        for n, rs in enumerate(rows):
            u_v, w_k, qk = pb[n]
            r = _dot(jnp.concatenate([w_k, scr["bqg"][rs, :]], axis=0), _bf(s))
            yield
            u = u_v - r[0:CHUNK]
            ubd = _blockdiag(_bf(u), bd_std[...])
            o = r[CHUNK:2 * CHUNK] + _dot(qk, ubd)
            kv = _dot_tn(_blockdiag(scr["bkt"][rs, :], bd_std[...]), ubd)
            yield
            s = s * scr["bcd"][rs.start:rs.start + 1, :] + kv
            oscr[rs, GROUP:2 * GROUP] = o
        scr["sb"][...] = s

    def gl_p2(store, qx_ref, kx_ref, v_ref, cd_ref, bd_k, st_ref, col):
        st = st_ref[...]
        for n, rs in enumerate(rows):
            vbd = _blockdiag(v_ref[rs, :], bd_std[...])
            o = _dot(store[n], vbd) + _dot_nt(qx_ref[GL_EDGE, rs, :], _bf(st))
            kv = _dot_tn(vbd, _blockdiag(kx_ref[GL_EDGE, rs, :], bd_k[...]))
            yield
            st = st * cd_ref[rs.stop - 1:rs.stop, :] + kv
            oscr[rs, col:col + GROUP] = o
        st_ref[...] = st

    def finish(idx):
        inv_hd = 1.0 / HEAD_DIM
        gains = (None, par["gdn_nw"], par["gla_nw"], par["hg_nw"])
        gates = (Z_RG, Z_BG, Z_CG, Z_DG)
        for i in idx:
            o = oscr[:, i * GROUP:(i + 1) * GROUP]
            o = o * lax.rsqrt(seg_sum(o * o) * inv_hd + EPS)
            yield
            if gains[i] is not None:
                o = o * gains[i][...]
            o_ref[:, i * GROUP:(i + 1) * GROUP] = _bf(o * _silu(zcols(gates[i], GROUP)))

    _run_interleaved([g for n in range(n_chunks) for g in (gdn_p1(n), ret_p1(n))]
                     + [prep_c, prep_d])
    _run_interleaved([g for n in range(n_chunks)
                      for g in (gl_p1(n, scr["cqx"], scr["ckx"], cst["bd_gla"], pc),
                                gl_p1(n, scr["dqx"], scr["dkx"], bd_std, pd))]
                     + [gdn_p2(), ret_p2()])
    _run_interleaved([
        gl_p2(pc, scr["cqx"], scr["ckx"], scr["cv"], scr["ccd"], cst["bd_gla"], scr["sct"],
              2 * GROUP),
        gl_p2(pd, scr["dqx"], scr["dkx"], scr["dv"], scr["dcd"], bd_std, scr["sdt"],
              3 * GROUP),
        finish((0, 1))])
    _run_interleaved([finish((2, 3))])


def _mixer(zb, zf, cos_t, sin_t, params, batch, tb):
    m = zb.shape[0]
    t = m // batch
    nt = t // tb
    consts = {k: jnp.asarray(v, dt) for k, (v, dt) in _mixer_constants(tb).items()}
    const_args = [consts[k] for k in _CONST_ORDER]
    param_args = [params[k] for k in _PARAM_ORDER]

    def full_spec(a):
        nd = a.ndim
        return pl.BlockSpec(a.shape, lambda b, i, _nd=nd: (0,) * _nd)

    row_map = lambda b, i: (b * nt + i, 0)
    body = functools.partial(_mixer_body, n_chunks=tb // CHUNK)
    return pl.pallas_call(
        body,
        out_shape=jax.ShapeDtypeStruct((m, N_HEADS * GROUP), BF16),
        grid=(batch, nt),
        in_specs=[pl.BlockSpec((tb, Z_WIDTH), row_map),
                  pl.BlockSpec((tb, F_WIDTH), row_map),
                  pl.BlockSpec((tb, LANES), row_map),
                  pl.BlockSpec((tb, LANES), row_map)]
                 + [full_spec(a) for a in const_args]
                 + [full_spec(a) for a in param_args],
        out_specs=pl.BlockSpec((tb, N_HEADS * GROUP), row_map),
        scratch_shapes=_mixer_scratch(tb),
        compiler_params=pltpu.CompilerParams(
            dimension_semantics=("parallel", "arbitrary"),
            vmem_limit_bytes=VMEM_LIMIT),
        name="mixer",
    )(zb, zf, cos_t, sin_t, *const_args, *param_args)


def _in_proj_segments():
    src = {}
    off = 0
    for name, width in (("rq", 256), ("rk", 256), ("rv", 256), ("rg", 256),
                        ("bq", 256), ("bk", 256), ("bv", 256), ("ba", 4), ("bb", 4), ("bg", 256),
                        ("cq", 128), ("ck", 128), ("cv", 256), ("clr", 16), ("cg", 256),
                        ("dq", 256), ("df", 256), ("di", 256), ("dg", 256)):
        src[name] = off
        off += width

    def rope(name):
        return [(src[name] + h * HEAD_DIM + half * ROPE_HALF, ROPE_HALF)
                for half in range(2) for h in range(N_HEADS)]

    segs = rope("rq") + rope("rk")
    segs += [(src["rv"], 256), (src["rg"], 256), (src["bq"], 768), (src["bg"], 256),
             (src["cq"], 128), (src["ck"], 128), (src["cv"], 256), (src["cg"], 256),
             (src["dq"], 256), (src["di"], 256), (src["dg"], 256)]
    assert sum(w for _, w in segs) == Z_WIDTH
    segs += [(src["df"], 256), (src["ba"], 8), (src["clr"], GLA_LOWRANK),
             (None, LANES - SM_LR - GLA_LOWRANK)]
    assert sum(w for _, w in segs) == Z_WIDTH + F_WIDTH
    return segs


def kernel(x, mem, positions, mix_norm_w, w_in, gdn_conv_w, gdn_a_log, gdn_dt_bias, gdn_norm_w,
           gla_gk_up, gla_gk_bias, gla_norm_w, hgrn_lb_logits, hgrn_norm_w, w_out,
           xattn_norm_w, mem_norm_w, xattn_wq, xattn_wk, xattn_wv, xattn_wo,
           ffn_norm_w, ffn_up, ffn_conv_w, ffn_down, final_norm_w):
    batch, seq, d = x.shape
    depth = w_in.shape[0]
    m = batch * seq
    mem_len = mem.shape[1]

    w_in_p = jnp.concatenate(
        [jnp.zeros((depth, d, wd), BF16) if s0 is None else w_in[:, :, s0:s0 + wd].astype(BF16)
         for s0, wd in _in_proj_segments()], axis=2)

    freqs = ROPE_BASE ** (-jnp.arange(0, HEAD_DIM, 2, dtype=F32) / HEAD_DIM)
    freq_row = jnp.tile(freqs, N_HEADS).reshape(1, LANES)
    cos_t, sin_t = _rope_tables(positions.reshape(m, 1), freq_row, 1024)

    lb_all = jnp.cumsum(jax.nn.softmax(hgrn_lb_logits.astype(F32), axis=0), axis=0)
    lb_all = lb_all - lb_all[0]

    def head_row(v):
        return jnp.tile(v.astype(F32), N_HEADS).reshape(1, GROUP)

    w_out_b, wq_b, wo_b = (w.astype(BF16) for w in (w_out, xattn_wq, xattn_wo))
    w_kv_b = jnp.concatenate([xattn_wk, xattn_wv], axis=2).astype(BF16)
    w_up_b, w_down_b = ffn_up.astype(BF16), ffn_down.astype(BF16)
    d_ff = ffn_down.shape[1]
    conv_w_all = ffn_conv_w.astype(F32) * jnp.where(jnp.arange(2 * d_ff) < d_ff, 0.5, 1.0)

    h = x.reshape(m, d)
    mem2 = mem.reshape(batch * mem_len, d)
    for l in range(depth):
        gla_up = jnp.zeros((LANES, GLA_KEYS), F32).at[SM_LR:SM_LR + GLA_LOWRANK].set(gla_gk_up[l])
        params = {
            "gdn_conv": gdn_conv_w[l].astype(F32),
            "gdn_alog": jnp.repeat(gdn_a_log[l].astype(F32), HEAD_DIM).reshape(1, GROUP),
            "gdn_dt": jnp.repeat(gdn_dt_bias[l].astype(F32), HEAD_DIM).reshape(1, GROUP),
            "gdn_nw": head_row(gdn_norm_w[l]),
            "gla_up": gla_up,
            "gla_bias": gla_gk_bias[l].astype(F32).reshape(1, GLA_KEYS),
            "gla_nw": head_row(gla_norm_w[l]),
            "hg_loglb": jnp.log(lb_all[l]).reshape(1, GROUP),
            "hg_log1mlb": jnp.log1p(-lb_all[l]).reshape(1, GROUP),
            "hg_1mlb": (1.0 - lb_all[l]).reshape(1, GROUP),
            "hg_nw": head_row(hgrn_norm_w[l]),
        }
        zb, zf = _norm_matmul(h, mix_norm_w[l], w_in_p[l],
                              ((Z_WIDTH, BF16), (F_WIDTH, F32)), TM_IN_PROJ)
        o = _mixer(zb, zf, cos_t, sin_t, params, batch, TB_MIXER)

        kv, = _norm_matmul(mem2, mem_norm_w, w_kv_b[l], ((2 * d, BF16),), TM_KV)
        h = _xattn(o, w_out_b[l], h, xattn_norm_w[l], wq_b[l],
                   kv.reshape(batch, mem_len, 2 * d), wo_b[l], batch, TQ_XATTN)

        u, = _norm_matmul(h, ffn_norm_w[l], w_up_b[l], ((ffn_up.shape[2], BF16),), TM_FFN_UP)
        h = _conv_ffn_down(u, conv_w_all[l], w_down_b[l], h,
                           final_norm_w, seq, TM_FFN_DOWN, l == depth - 1)
    return h.reshape(batch, seq, d)
```
